```python
import jax, jax.numpy as jnp
from jax import lax
import numpy as np

D_MODEL = 1024
BATCH = 2
SEQ = 8192
DEPTH = 2
DEC_BATCH = 32
DEC_SEQ = 1
PAST_LEN = 16384
PAGE_SIZE = 128

D_POOL = D_MODEL // 2
POOL_WINDOWS = (2, 4, 8, 16)
N_POOL_GROUPS = len(POOL_WINDOWS)
POOL_GROUP = D_POOL // N_POOL_GROUPS
POOL_STATE = max(POOL_WINDOWS) - 1
D_ATTN = D_MODEL - D_POOL
HEAD_DIM = 64
N_HEADS = D_ATTN // HEAD_DIM
Q_BLOCK = 128
D_IN = D_POOL + 3 * D_ATTN + N_HEADS
FORGET_BIAS = 3.0
PEER_HEADS = 8
PEER_TOPK = 16
N_KEYS = 128
N_EXPERTS = N_KEYS * N_KEYS
D_QUERY = 256
PEER_BLOCK = 128
EPS = 1e-6

kernel_name = "hymba_pool_fox_peer_step"


def rmsnorm(x, g):
    xf = x.astype(jnp.float32)
    y = xf * lax.rsqrt(jnp.mean(xf * xf, axis=-1, keepdims=True) + EPS) * g.astype(jnp.float32)
    return y.astype(x.dtype)


def project(h, w_in, b_forget, q_norm, k_norm):
    b, t, _ = h.shape
    z = h @ w_in
    xp = z[..., :D_POOL]
    q = z[..., D_POOL:D_POOL + D_ATTN].reshape(b, t, N_HEADS, HEAD_DIM)
    k = z[..., D_POOL + D_ATTN:D_POOL + 2 * D_ATTN].reshape(b, t, N_HEADS, HEAD_DIM)
    v = z[..., D_POOL + 2 * D_ATTN:D_POOL + 3 * D_ATTN].reshape(b, t, N_HEADS, HEAD_DIM)
    f_logit = (z[..., D_POOL + 3 * D_ATTN:] + b_forget).astype(jnp.float32)
    logf = jax.nn.log_sigmoid(f_logit)
    return xp, rmsnorm(q, q_norm), rmsnorm(k, k_norm), v, logf


def pool_mix(xp, state, start, pool_w, pool_scale):
    b, t, _ = xp.shape
    xx = jnp.concatenate([state.astype(xp.dtype), xp], axis=1)
    c = jnp.pad(jnp.cumsum(xx.astype(jnp.float32), axis=1), ((0, 0), (1, 0), (0, 0)))
    L = POOL_STATE + 1
    pos = start + jnp.arange(t)
    outs = []
    for g, w in enumerate(POOL_WINDOWS):
        sl = slice(g * POOL_GROUP, (g + 1) * POOL_GROUP)
        wsum = c[:, L:L + t, sl] - c[:, L - w:L - w + t, sl]
        cnt = jnp.minimum(pos + 1, w).astype(jnp.float32)[None, :, None]
        outs.append(wsum / cnt - xp[..., sl].astype(jnp.float32))
    pooled = jnp.stack(outs, axis=2)
    mixed = jnp.einsum('btgc,gcd->btgd', pooled, pool_w.astype(jnp.float32)).reshape(b, t, D_POOL)
    mixed = mixed * pool_scale.astype(jnp.float32)
    return mixed.astype(xp.dtype), xx[:, -POOL_STATE:]


def fox_prompt(q, k, v, logf):
    B, S, H, dh = q.shape
    nb = S // Q_BLOCK
    scale = dh ** -0.5
    Fk = jnp.cumsum(logf, axis=1).transpose(0, 2, 1)
    kf = k.astype(jnp.float32)
    vf = v.astype(jnp.float32)
    qb = q.astype(jnp.float32).reshape(B, nb, Q_BLOCK, H, dh).transpose(1, 0, 2, 3, 4)
    Fq = Fk.reshape(B, H, nb, Q_BLOCK).transpose(2, 0, 1, 3)
    kpos = jnp.arange(S)

    def block(args):
        qi, Fi, i = args
        s = jnp.einsum('bqhd,bkhd->bhqk', qi, kf) * scale + Fi[..., None] - Fk[:, :, None, :]
        qpos = i * Q_BLOCK + jnp.arange(Q_BLOCK)
        s = jnp.where(qpos[:, None] >= kpos[None, :], s, -jnp.inf)
        p = jax.nn.softmax(s, axis=-1)
        return jnp.einsum('bhqk,bkhd->bqhd', p, vf)

    o = lax.map(block, (qb, Fq, jnp.arange(nb)))
    return o.transpose(1, 0, 2, 3, 4).reshape(B, S, H * dh)


def fox_sample(q, k, v, logf, k_past, v_past, logf_past):
    b, tn, H, dh = q.shape
    P = k_past.shape[1]
    scale = dh ** -0.5
    qf = q.astype(jnp.float32)
    lp = logf_past.astype(jnp.float32)
    r_past = lax.cumsum(lp, axis=1, reverse=True) - lp
    c_new = jnp.cumsum(logf, axis=1).transpose(0, 2, 1)
    sp = (jnp.einsum('bqhd,bkhd->bhqk', qf, k_past.astype(jnp.float32)) * scale
          + r_past.transpose(0, 2, 1)[:, :, None, :] + c_new[..., None])
    sn = (jnp.einsum('bqhd,bkhd->bhqk', qf, k.astype(jnp.float32)) * scale
          + c_new[..., None] - c_new[:, :, None, :])
    tri = jnp.arange(tn)[:, None] >= jnp.arange(tn)[None, :]
    sn = jnp.where(tri, sn, -jnp.inf)
    p = jax.nn.softmax(jnp.concatenate([sp, sn], axis=-1), axis=-1)
    o = (jnp.einsum('bhqk,bkhd->bqhd', p[..., :P], v_past.astype(jnp.float32))
         + jnp.einsum('bhqk,bkhd->bqhd', p[..., P:], v.astype(jnp.float32)))
    return o.reshape(b, tn, H * dh)


def peer(h, wq, subkeys, u_tab, v_tab):
    b, t, d = h.shape
    xt = h.reshape(b * t, d)
    n = b * t
    blk = min(PEER_BLOCK, n)
    nb = -(-n // blk)
    xt = jnp.pad(xt, ((0, nb * blk - n), (0, 0))).reshape(nb, blk, d)
    sk = subkeys.astype(jnp.float32)

    def block(xb):
        qv = (xb @ wq).reshape(blk, PEER_HEADS, 2, D_QUERY // 2).astype(jnp.float32)
        s = jnp.einsum('nhpc,pkc->nhpk', qv, sk)
        sv, si = lax.top_k(s, PEER_TOPK)
        cand = sv[:, :, 0, :, None] + sv[:, :, 1, None, :]
        cidx = si[:, :, 0, :, None] * N_KEYS + si[:, :, 1, None, :]
        top_s, top_j = lax.top_k(cand.reshape(blk, PEER_HEADS, PEER_TOPK * PEER_TOPK), PEER_TOPK)
        eidx = jnp.take_along_axis(cidx.reshape(blk, PEER_HEADS, PEER_TOPK * PEER_TOPK), top_j, axis=-1)
        g = jax.nn.softmax(top_s, axis=-1)
        u = jnp.take(u_tab, eidx, axis=0)
        a = jax.nn.gelu(jnp.einsum('nhkd,nd->nhk', u, xb).astype(jnp.float32))
        vv = jnp.take(v_tab, eidx, axis=0)
        return jnp.einsum('nhk,nhkd->nd', (g * a).astype(xb.dtype), vv)

    out = lax.map(block, xt).reshape(nb * blk, d)[:n]
    return out.reshape(b, t, d)


def layer_forward(x, pool_state, start, attend, lw):
    (norm_mix, w_in, b_forget, q_norm, k_norm, pool_w, pool_scale, w_out,
     norm_ffn, peer_wq, peer_subkeys, peer_u, peer_v) = lw
    h = rmsnorm(x, norm_mix)
    xp, q, k, v, logf = project(h, w_in, b_forget, q_norm, k_norm)
    pool_out, new_pool = pool_mix(xp, pool_state, start, pool_w, pool_scale)
    attn_out = attend(q, k, v, logf).astype(x.dtype)
    x = x + jnp.concatenate([pool_out, attn_out], axis=-1) @ w_out
    x = x + peer(rmsnorm(x, norm_ffn), peer_wq, peer_subkeys, peer_u, peer_v)
    return x, new_pool, k, v, logf


def setup_inputs(seed: int = 0) -> dict:
    key = jax.random.key(seed)
    ks = jax.random.split(key, 24)
    n_pages = PAST_LEN // PAGE_SIZE
    n_used = DEC_BATCH * n_pages
    n_pool = n_used + (n_used + 3) // 4
    nrm = jax.random.normal
    f32 = jnp.float32
    page_table = jax.random.permutation(ks[0], n_pool)[:n_used].reshape(DEC_BATCH, n_pages).astype(jnp.int32)
    return {
        "x_prompt": nrm(ks[1], (BATCH, SEQ, D_MODEL), f32),
        "x_sample": nrm(ks[2], (DEC_BATCH, DEC_SEQ, D_MODEL), f32),
        "cache_k": nrm(ks[3], (DEPTH, n_pool, PAGE_SIZE, N_HEADS, HEAD_DIM), f32),
        "cache_v": nrm(ks[4], (DEPTH, n_pool, PAGE_SIZE, N_HEADS, HEAD_DIM), f32),
        "cache_logf": jax.nn.log_sigmoid(FORGET_BIAS + 0.5 * nrm(ks[5], (DEPTH, n_pool, PAGE_SIZE, N_HEADS), f32)),
        "state_pool": nrm(ks[6], (DEPTH, DEC_BATCH, POOL_STATE, D_POOL), f32),
        "page_table": page_table,
        "norm_mix": 1.0 + 0.02 * nrm(ks[7], (DEPTH, D_MODEL), f32),
        "w_in": nrm(ks[8], (DEPTH, D_MODEL, D_IN), f32) * D_MODEL ** -0.5,
        "b_forget": FORGET_BIAS + 0.1 * nrm(ks[9], (DEPTH, N_HEADS), f32),
        "q_norm": 1.0 + 0.02 * nrm(ks[10], (DEPTH, HEAD_DIM), f32),
        "k_norm": 1.0 + 0.02 * nrm(ks[11], (DEPTH, HEAD_DIM), f32),
        "pool_w": nrm(ks[12], (DEPTH, N_POOL_GROUPS, POOL_GROUP, POOL_GROUP), f32) * POOL_GROUP ** -0.5,
        "pool_scale": 1.0 + 0.02 * nrm(ks[13], (DEPTH, D_POOL), f32),
        "w_out": nrm(ks[14], (DEPTH, D_MODEL, D_MODEL), f32) * D_MODEL ** -0.5,
        "norm_ffn": 1.0 + 0.02 * nrm(ks[15], (DEPTH, D_MODEL), f32),
        "peer_wq": nrm(ks[16], (DEPTH, D_MODEL, PEER_HEADS * D_QUERY), f32) * D_MODEL ** -0.5,
        "peer_subkeys": nrm(ks[17], (DEPTH, 2, N_KEYS, D_QUERY // 2), f32) * (D_QUERY // 2) ** -0.5,
        "peer_u": nrm(ks[18], (DEPTH, N_EXPERTS, D_MODEL), f32) * D_MODEL ** -0.5,
        "peer_v": nrm(ks[19], (DEPTH, N_EXPERTS, D_MODEL), f32) * PEER_HEADS ** -0.5,
    }


def reference(x_prompt, x_sample, cache_k, cache_v, cache_logf, state_pool, page_table,
              norm_mix, w_in, b_forget, q_norm, k_norm, pool_w, pool_scale, w_out,
              norm_ffn, peer_wq, peer_subkeys, peer_u, peer_v):
    n_seq = page_table.shape[0]
    past = page_table.shape[1] * PAGE_SIZE
    yp, ys = x_prompt, x_sample
    kp_l, vp_l, fp_l, pp_l = [], [], [], []
    ks_l, vs_l, fs_l, ps_l = [], [], [], []
    for l in range(DEPTH):
        lw = (norm_mix[l], w_in[l], b_forget[l], q_norm[l], k_norm[l], pool_w[l], pool_scale[l],
              w_out[l], norm_ffn[l], peer_wq[l], peer_subkeys[l], peer_u[l], peer_v[l])
        zero_pool = jnp.zeros((yp.shape[0], POOL_STATE, D_POOL), yp.dtype)
        yp, pp, kp, vp, fp = layer_forward(yp, zero_pool, 0, fox_prompt, lw)
        k_past = cache_k[l, page_table].reshape(n_seq, past, N_HEADS, HEAD_DIM)
        v_past = cache_v[l, page_table].reshape(n_seq, past, N_HEADS, HEAD_DIM)
        f_past = cache_logf[l, page_table].reshape(n_seq, past, N_HEADS)
        attend = lambda q, k, v, f, kp_=k_past, vp_=v_past, fp_=f_past: fox_sample(q, k, v, f, kp_, vp_, fp_)
        ys, ps, ksn, vsn, fsn = layer_forward(ys, state_pool[l], past, attend, lw)
        kp_l.append(kp); vp_l.append(vp); fp_l.append(fp.astype(cache_logf.dtype)); pp_l.append(pp)
        ks_l.append(ksn); vs_l.append(vsn); fs_l.append(fsn.astype(cache_logf.dtype)); ps_l.append(ps)
    return (yp, ys,
            jnp.stack(kp_l), jnp.stack(vp_l), jnp.stack(fp_l), jnp.stack(pp_l),
            jnp.stack(ks_l), jnp.stack(vs_l), jnp.stack(fs_l), jnp.stack(ps_l))
```

```python
import functools

import jax
import jax.numpy as jnp
from jax import lax
from jax.experimental import pallas as pl
from jax.experimental.pallas import tpu as pltpu

F32 = jnp.float32
BF16 = jnp.bfloat16

EPS = 1e-6
D_MODEL = 1024
D_POOL = 512
POOL_WINDOWS = (2, 4, 8, 16)
POOL_GROUP = 128
POOL_STATE = 15
N_HEADS = 8
HEAD_DIM = 64
D_ATTN = N_HEADS * HEAD_DIM
PAGE_SIZE = 128
PEER_HEADS = 8
PEER_TOPK = 16
N_KEYS = 128
D_HALF = 128
ATTN_SCALE = HEAD_DIM ** -0.5

LANES = 128
AUG = 128
TOK_TILE = 512
SEQ_TILE = 256
Q_TILE = 256
KV_TILE = 512
EXP_ROWS = 8
PAGES_PER_STEP = 8
VMEM_LIMIT = 56 * 1024 * 1024
NEG = -1e30


def _cparams(sem):
    return pltpu.CompilerParams(dimension_semantics=sem, vmem_limit_bytes=VMEM_LIMIT)


def _split3(x):
    hi = x.astype(BF16)
    r = x - hi.astype(F32)
    mid = r.astype(BF16)
    lo = (r - mid.astype(F32)).astype(BF16)
    return hi, mid, lo


def _dot(a, b):
    return jnp.dot(a, b, preferred_element_type=F32)


def _dot_nt(a, b):
    return lax.dot_general(a, b, (((1,), (1,)), ((), ())), preferred_element_type=F32)


def _inproj_body(x_ref, g_ref, wm_ref, wf_ref, bf_ref, qg_ref, kg_ref, hm_ref,
                 xp_ref, q_ref, k_ref, v_ref, lf_ref):
    x = x_ref[...]
    ms = jnp.mean(x * x, axis=-1, keepdims=True)
    h = (x * lax.rsqrt(ms + EPS) * g_ref[...]).astype(BF16)
    z = _dot(h, wm_ref[...])
    xp_ref[...] = z[:, 0:D_POOL]
    hm = hm_ref[...]

    def headnorm(t, gain):
        sq = t * t
        hi = sq.astype(BF16)
        lo = (sq - hi.astype(F32)).astype(BF16)
        msh = _dot(hi, hm) + _dot(lo, hm)
        return t * lax.rsqrt(msh + EPS) * gain

    q_ref[...] = headnorm(z[:, D_POOL:D_POOL + D_ATTN], qg_ref[...])
    k_ref[...] = headnorm(z[:, D_POOL + D_ATTN:D_POOL + 2 * D_ATTN], kg_ref[...])
    v_ref[...] = z[:, D_POOL + 2 * D_ATTN:D_POOL + 3 * D_ATTN]
    f = _dot(h, wf_ref[...]) + bf_ref[...]
    lf_ref[...] = jnp.minimum(f, 0.0) - jnp.log1p(jnp.exp(-jnp.abs(f)))


def _inproj(x, g, wm, wf, bfp, qg, kg, hm):
    n = x.shape[0]
    t = TOK_TILE
    row = lambda i: (i, 0)
    fix = lambda i: (0, 0)
    return pl.pallas_call(
        _inproj_body,
        grid=(n // t,),
        in_specs=[
            pl.BlockSpec((t, D_MODEL), row),
            pl.BlockSpec((1, D_MODEL), fix),
            pl.BlockSpec(wm.shape, fix),
            pl.BlockSpec(wf.shape, fix),
            pl.BlockSpec((1, LANES), fix),
            pl.BlockSpec((1, D_ATTN), fix),
            pl.BlockSpec((1, D_ATTN), fix),
            pl.BlockSpec(hm.shape, fix),
        ],
        out_specs=[
            pl.BlockSpec((t, D_POOL), row),
            pl.BlockSpec((t, D_ATTN), row),
            pl.BlockSpec((t, D_ATTN), row),
            pl.BlockSpec((t, D_ATTN), row),
            pl.BlockSpec((t, LANES), row),
        ],
        out_shape=[
            jax.ShapeDtypeStruct((n, D_POOL), F32),
            jax.ShapeDtypeStruct((n, D_ATTN), F32),
            jax.ShapeDtypeStruct((n, D_ATTN), F32),
            jax.ShapeDtypeStruct((n, D_ATTN), F32),
            jax.ShapeDtypeStruct((n, LANES), F32),
        ],
        compiler_params=_cparams(("arbitrary",)),
        name="inproj",
    )(x, g, wm, wf, bfp, qg, kg, hm)


def _pool_mix(xp, window_sum, cnt_fn, pw_ref, ps_ref):
    outs = []
    for g, w in enumerate(POOL_WINDOWS):
        lanes = slice(g * POOL_GROUP, (g + 1) * POOL_GROUP)
        pooled = window_sum(g, w) / cnt_fn(w) - xp[:, lanes]
        outs.append(_dot(pooled.astype(BF16), pw_ref[g]))
    return jnp.concatenate(outs, axis=1) * ps_ref[...]


def _poolprep_body(xp_ref, lf_ref, q_ref, k_ref, v_ref, pw_ref, ps_ref,
                   selq_ref, selk_ref, selv_ref,
                   po_ref, qa_ref, ka_ref, vt_ref, xx_ref, fc_ref):
    t = SEQ_TILE
    hist = 16
    step = pl.program_id(1)

    @pl.when(step == 0)
    def _():
        xx_ref[0:hist, :] = jnp.zeros((hist, D_POOL), F32)
        fc_ref[...] = jnp.zeros_like(fc_ref)

    xp = xp_ref[...]
    xx_ref[hist:hist + t, :] = xp
    pos = lax.broadcasted_iota(jnp.int32, (t, POOL_GROUP), 0) + step * t

    def window_sum(g, w):
        lanes = slice(g * POOL_GROUP, (g + 1) * POOL_GROUP)
        ws = xp[:, lanes]
        for r in range(1, w):
            ws = ws + xx_ref[hist - r:hist - r + t, lanes]
        return ws

    def cnt(w):
        return jnp.minimum(pos + 1, w).astype(F32)

    po_ref[...] = _pool_mix(xp, window_sum, cnt, pw_ref, ps_ref)
    xx_ref[0:hist, :] = xx_ref[t:t + hist, :]

    ri = lax.broadcasted_iota(jnp.int32, (t, t), 0)
    ci = lax.broadcasted_iota(jnp.int32, (t, t), 1)
    tri = jnp.where(ci <= ri, 1.0, 0.0).astype(BF16)
    hi, mid, lo = _split3(lf_ref[...])
    fcum = _dot(tri, hi) + _dot(tri, mid) + _dot(tri, lo) + fc_ref[...]
    fc_ref[...] = fcum[t - 1:t, :]
    fh, fm, fl = _split3(fcum)
    ones = jnp.ones((t, LANES), BF16)
    wq = jnp.concatenate([q_ref[...].astype(BF16), fh, fm, fl, ones], axis=1)
    wk = jnp.concatenate([k_ref[...].astype(BF16), fh, fm, fl, ones], axis=1)
    wv = jnp.concatenate([v_ref[...].astype(BF16), ones], axis=1)
    for h in range(N_HEADS):
        qa_ref[h] = _dot(wq, selq_ref[h]).astype(BF16)
        ka_ref[h] = _dot(wk, selk_ref[h]).astype(BF16)
        vt_ref[h] = _dot_nt(selv_ref[h], wv).astype(BF16)


def _poolprep(xp, lf, q, k, v, pw, ps, selq, selk, selv, batch, seq):
    t = SEQ_TILE
    nt = seq // t
    n = batch * seq
    row = lambda b, i: (b * nt + i, 0)
    fix2 = lambda b, i: (0, 0)
    fix3 = lambda b, i: (0, 0, 0)
    return pl.pallas_call(
        _poolprep_body,
        grid=(batch, nt),
        in_specs=[
            pl.BlockSpec((t, D_POOL), row),
            pl.BlockSpec((t, LANES), row),
            pl.BlockSpec((t, D_ATTN), row),
            pl.BlockSpec((t, D_ATTN), row),
            pl.BlockSpec((t, D_ATTN), row),
            pl.BlockSpec(pw.shape, fix3),
            pl.BlockSpec((1, D_POOL), fix2),
            pl.BlockSpec(selq.shape, fix3),
            pl.BlockSpec(selk.shape, fix3),
            pl.BlockSpec(selv.shape, fix3),
        ],
        out_specs=[
            pl.BlockSpec((t, D_POOL), row),
            pl.BlockSpec((N_HEADS, t, AUG), lambda b, i: (0, b * nt + i, 0)),
            pl.BlockSpec((N_HEADS, t, AUG), lambda b, i: (0, b * nt + i, 0)),
            pl.BlockSpec((N_HEADS, AUG, t), lambda b, i: (0, 0, b * nt + i)),
        ],
        out_shape=[
            jax.ShapeDtypeStruct((n, D_POOL), F32),
            jax.ShapeDtypeStruct((N_HEADS, n, AUG), BF16),
            jax.ShapeDtypeStruct((N_HEADS, n, AUG), BF16),
            jax.ShapeDtypeStruct((N_HEADS, AUG, n), BF16),
        ],
        scratch_shapes=[
            pltpu.VMEM((t + 16, D_POOL), F32),
            pltpu.VMEM((1, LANES), F32),
        ],
        compiler_params=_cparams(("arbitrary", "arbitrary")),
        name="poolprep",
    )(xp, lf, q, k, v, pw, ps, selq, selk, selv)


def _flash_body(qa_ref, ka_ref, vt_ref, o_ref):
    tq, tk = Q_TILE, KV_TILE
    i = pl.program_id(2)
    n_full = (i * tq) // tk
    qpos = lax.broadcasted_iota(jnp.int32, (tk, tq), 1) + i * tq
    krel = lax.broadcasted_iota(jnp.int32, (tk, tq), 0)
    outs = []
    for hh in range(2):
        q = qa_ref[hh]

        def tile(j, carry, masked):
            m, acc = carry
            start = pl.multiple_of(j * tk, tk)
            s = _dot_nt(ka_ref[hh, pl.ds(start, tk), :], q)
            if masked:
                s = jnp.where(krel + j * tk <= qpos, s, NEG)
            m_new = jnp.maximum(m, jnp.max(s, axis=0, keepdims=True))
            p = jnp.exp(s - m_new).astype(BF16)
            alpha = jnp.exp(m - m_new)
            acc = alpha * acc + _dot(vt_ref[hh, :, pl.ds(start, tk)], p)
            return m_new, acc

        init = (jnp.full((1, tq), NEG, F32), jnp.zeros((AUG, tq), F32))
        carry = lax.fori_loop(0, n_full, lambda j, c: tile(j, c, False), init)
        _, acc = tile(n_full, carry, True)
        acc_t = acc.T
        outs.append(acc_t[:, 0:HEAD_DIM] / acc_t[:, HEAD_DIM:HEAD_DIM + 1])
    o_ref[...] = jnp.concatenate(outs, axis=1)


def _flash(qa, ka, vt, batch, seq):
    tq = Q_TILE
    nq = seq // tq
    n = batch * seq
    return pl.pallas_call(
        _flash_body,
        grid=(batch, N_HEADS // 2, nq),
        in_specs=[
            pl.BlockSpec((2, tq, AUG), lambda b, hp, i: (hp, b * nq + i, 0)),
            pl.BlockSpec((2, seq, AUG), lambda b, hp, i: (hp, b, 0)),
            pl.BlockSpec((2, AUG, seq), lambda b, hp, i: (hp, 0, b)),
        ],
        out_specs=pl.BlockSpec((tq, 2 * HEAD_DIM), lambda b, hp, i: (b * nq + i, hp)),
        out_shape=jax.ShapeDtypeStruct((n, D_ATTN), F32),
        compiler_params=_cparams(("arbitrary", "arbitrary", "arbitrary")),
        name="flash",
    )(qa, ka, vt)


def _sample_pool_body(xp_ref, st_ref, pw_ref, ps_ref, o_ref, *, start):
    xp = xp_ref[...]

    def window_sum(g, w):
        lanes = slice(g * POOL_GROUP, (g + 1) * POOL_GROUP)
        ws = xp[:, lanes]
        for r in range(1, w):
            ws = ws + st_ref[POOL_STATE - r][:, lanes]
        return ws

    o_ref[...] = _pool_mix(xp, window_sum, lambda w: float(min(start + 1, w)), pw_ref, ps_ref)


def _sample_pool(xp_s, state_t, pw, ps, start):
    return pl.pallas_call(
        functools.partial(_sample_pool_body, start=start),
        out_shape=jax.ShapeDtypeStruct(xp_s.shape, F32),
        compiler_params=pltpu.CompilerParams(vmem_limit_bytes=VMEM_LIMIT),
        name="sample_pool",
    )(xp_s, state_t, pw, ps)


def _sample_attn_body(pt_ref, q_ref, k_ref, v_ref, lf_ref, sfx_ref, *refs, n_steps):
    g_pages = PAGES_PER_STEP
    ck = refs[0:g_pages]
    cv = refs[g_pages:2 * g_pages]
    cl = refs[2 * g_pages:3 * g_pages]
    o_ref = refs[3 * g_pages]
    m_ref, l_ref, acc_ref, car_ref, lfb_ref = refs[3 * g_pages + 1:]
    rows = 2 * N_HEADS
    step = pl.program_id(1)

    hrow = lax.broadcasted_iota(jnp.int32, (rows, D_ATTN), 0)
    hcol = lax.broadcasted_iota(jnp.int32, (rows, D_ATTN), 1) // HEAD_DIM
    headmask = hrow == hcol
    qf = q_ref[0].astype(BF16).astype(F32)
    qbd = jnp.where(headmask, qf * ATTN_SCALE, 0.0)

    @pl.when(step == 0)
    def _():
        m_ref[...] = jnp.full(m_ref.shape, NEG, F32)
        l_ref[...] = jnp.zeros_like(l_ref)
        acc_ref[...] = jnp.zeros_like(acc_ref)
        lfb_ref[...] = jnp.zeros_like(lfb_ref)
        eye = (lax.broadcasted_iota(jnp.int32, (rows, LANES), 0)
               == lax.broadcasted_iota(jnp.int32, (rows, LANES), 1))
        c_new = jnp.sum(jnp.where(eye, lf_ref[0], 0.0), axis=1, keepdims=True)
        car_ref[...] = jnp.broadcast_to(c_new, car_ref.shape)

    carry = car_ref[...]
    s_parts = []
    for g in range(g_pages):
        lfb_ref[:, 0:N_HEADS] = cl[g][...]
        lft = lfb_ref[...].T[0:rows, :]
        hi, mid, lo = _split3(lft)
        sfx = sfx_ref[...]
        r = _dot(hi, sfx) + _dot(mid, sfx) + _dot(lo, sfx)
        kb = ck[g][...].astype(BF16)
        s = _dot_nt(qbd.astype(BF16), kb) + r[:, 0:PAGE_SIZE] + carry
        s_parts.append(s)
        carry = carry + r[:, PAGE_SIZE:2 * PAGE_SIZE]
    car_ref[...] = carry
    s_all = jnp.concatenate(s_parts, axis=1)
    v_all = jnp.concatenate([cv[g][...].astype(BF16) for g in range(g_pages)], axis=0)
    m_old = m_ref[...]
    m_new = jnp.maximum(m_old, jnp.max(s_all, axis=1, keepdims=True))
    p = jnp.exp(s_all - m_new)
    alpha = jnp.exp(m_old - m_new)
    l_ref[...] = alpha * l_ref[...] + jnp.sum(p, axis=1, keepdims=True)
    acc_ref[...] = alpha * acc_ref[...] + _dot(p.astype(BF16), v_all)
    m_ref[...] = m_new

    @pl.when(step == n_steps - 1)
    def _():
        kf = k_ref[0].astype(BF16).astype(F32)
        vf = v_ref[0].astype(BF16).astype(F32)
        s_new = jnp.sum(qbd * kf, axis=1, keepdims=True)
        m_old = m_ref[...]
        m_fin = jnp.maximum(m_old, s_new)
        a = jnp.exp(m_old - m_fin)
        p_new = jnp.exp(s_new - m_fin)
        l_fin = a * l_ref[...] + p_new
        acc = a * acc_ref[...] + p_new.astype(BF16).astype(F32) * vf
        o_ref[0] = jnp.sum(jnp.where(headmask, acc / l_fin, 0.0), axis=0, keepdims=True)


def _sample_attn(page_table, q_s, k_s, v_s, lf_s, sfx, cache_k, cache_v, cache_lf, layer):
    n_seq, n_pages = page_table.shape
    g_pages = PAGES_PER_STEP
    n_steps = n_pages // g_pages
    rows = 2 * N_HEADS

    def page_map(g):
        return lambda b, s, pt: (layer, pt[b, n_pages - 1 - (s * g_pages + g)], 0, 0)

    tok = lambda b, s, pt: (b, 0, 0)
    in_specs = [
        pl.BlockSpec((1, 1, D_ATTN), tok),
        pl.BlockSpec((1, 1, D_ATTN), tok),
        pl.BlockSpec((1, 1, D_ATTN), tok),
        pl.BlockSpec((1, 1, LANES), tok),
        pl.BlockSpec(sfx.shape, lambda b, s, pt: (0, 0)),
    ]
    in_specs += [pl.BlockSpec((None, None, PAGE_SIZE, D_ATTN), page_map(g)) for g in range(g_pages)]
    in_specs += [pl.BlockSpec((None, None, PAGE_SIZE, D_ATTN), page_map(g)) for g in range(g_pages)]
    in_specs += [pl.BlockSpec((None, None, PAGE_SIZE, N_HEADS), page_map(g)) for g in range(g_pages)]
    grid_spec = pltpu.PrefetchScalarGridSpec(
        num_scalar_prefetch=1,
        grid=(n_seq, n_steps),
        in_specs=in_specs,
        out_specs=pl.BlockSpec((1, 1, D_ATTN), tok),
        scratch_shapes=[
            pltpu.VMEM((rows, 1), F32),
            pltpu.VMEM((rows, 1), F32),
            pltpu.VMEM((rows, D_ATTN), F32),
            pltpu.VMEM((rows, LANES), F32),
            pltpu.VMEM((PAGE_SIZE, LANES), F32),
        ],
    )
    return pl.pallas_call(
        functools.partial(_sample_attn_body, n_steps=n_steps),
        grid_spec=grid_spec,
        out_shape=jax.ShapeDtypeStruct((n_seq, 1, D_ATTN), F32),
        compiler_params=_cparams(("arbitrary", "arbitrary")),
        name="sample_attn",
    )(page_table, q_s, k_s, v_s, lf_s, sfx,
      *([cache_k] * g_pages), *([cache_v] * g_pages), *([cache_lf] * g_pages))


def _outproj_body(x_ref, po_ref, at_ref, wo_ref, g_ref, x1_ref, xn_ref):
    mix = jnp.concatenate([po_ref[...], at_ref[...]], axis=1).astype(BF16)
    x1 = x_ref[...] + _dot(mix, wo_ref[...])
    x1_ref[...] = x1
    ms = jnp.mean(x1 * x1, axis=-1, keepdims=True)
    xn_ref[...] = (x1 * lax.rsqrt(ms + EPS) * g_ref[...]).astype(BF16)


def _outproj(x, po, at, wo, g):
    n = x.shape[0]
    t = TOK_TILE
    row = lambda i: (i, 0)
    fix = lambda i: (0, 0)
    return pl.pallas_call(
        _outproj_body,
        grid=(n // t,),
        in_specs=[
            pl.BlockSpec((t, D_MODEL), row),
            pl.BlockSpec((t, D_POOL), row),
            pl.BlockSpec((t, D_ATTN), row),
            pl.BlockSpec(wo.shape, fix),
            pl.BlockSpec((1, D_MODEL), fix),
        ],
        out_specs=[pl.BlockSpec((t, D_MODEL), row), pl.BlockSpec((t, D_MODEL), row)],
        out_shape=[jax.ShapeDtypeStruct((n, D_MODEL), F32), jax.ShapeDtypeStruct((n, D_MODEL), BF16)],
        compiler_params=_cparams(("arbitrary",)),
        name="outproj",
    )(x, po, at, wo, g)


def _top16_ranks(s, sv_ref, half, rowi):
    rank = jnp.full(s.shape, float(PEER_TOPK), F32)
    for c in range(PEER_TOPK):
        m = jnp.max(s, axis=0, keepdims=True)
        first = jnp.min(jnp.where(s == m, rowi, float(N_KEYS)), axis=0, keepdims=True)
        sel = rowi == first
        rank = jnp.where(sel, float(c), rank)
        s = jnp.where(sel, -jnp.inf, s)
        sv_ref[half, c:c + 1, :] = m
    return rank


def _peer_route_body(xn_ref, wq_ref, sk_ref, a_ref, n_ref, b_ref, r_ref, q_scr, sv_ref):
    t = xn_ref.shape[0]
    q = _dot(xn_ref[...], wq_ref[...]).astype(BF16)
    for hp in range(2 * PEER_HEADS):
        q_scr[hp] = q[:, hp * D_HALF:(hp + 1) * D_HALF]
    rowi = lax.broadcasted_iota(jnp.int32, (N_KEYS, t), 0).astype(F32)

    blocks = []
    for c in range(PEER_TOPK):
        nd = PEER_TOPK // (c + 1)
        blocks.append((c, nd, -(-nd // 8) * 8))
    flat = jnp.concatenate(
        [lax.broadcasted_iota(jnp.int32, (ndp, t), 0).astype(F32) + float(c * PEER_TOPK)
         for c, nd, ndp in blocks], axis=0)
    valid = jnp.concatenate(
        [lax.broadcasted_iota(jnp.int32, (ndp, t), 0) < nd for c, nd, ndp in blocks], axis=0)
    big = float(PEER_TOPK * PEER_TOPK)

    def head(h, _):
        s1 = _dot_nt(sk_ref[0], q_scr[2 * h])
        s2 = _dot_nt(sk_ref[1], q_scr[2 * h + 1])
        r1 = _top16_ranks(s1, sv_ref, 0, rowi)
        r2 = _top16_ranks(s2, sv_ref, 1, rowi)
        sv1 = sv_ref[0]
        sv2 = sv_ref[1]
        e1 = jnp.exp(sv1 - sv1[0:1, :])
        e2 = jnp.exp(sv2 - sv2[0:1, :])
        cand = jnp.concatenate(
            [sv1[c:c + 1, :] + sv2[0:ndp, :] for c, nd, ndp in blocks], axis=0)
        cand = jnp.where(valid, cand, -jnp.inf)
        wgt = jnp.concatenate(
            [e1[c:c + 1, :] * e2[0:ndp, :] for c, nd, ndp in blocks], axis=0)
        chosen = jnp.zeros(cand.shape, F32)
        for _k in range(PEER_TOPK):
            m = jnp.max(cand, axis=0, keepdims=True)
            first = jnp.min(jnp.where(cand == m, flat, big), axis=0, keepdims=True)
            sel = flat == first
            chosen = jnp.where(sel, 1.0, chosen)
            cand = jnp.where(sel, -jnp.inf, cand)
        z = jnp.sum(chosen * wgt, axis=0, keepdims=True)
        nrow = jnp.zeros((N_KEYS, t), F32)
        off = 0
        for c, nd, ndp in blocks:
            n_c = jnp.sum(chosen[off:off + ndp, :], axis=0, keepdims=True)
            nrow = jnp.where(r1 == float(c), n_c, nrow)
            off += ndp
        a_ref[h] = jnp.where(r1 < float(PEER_TOPK), jnp.exp(s1 - sv1[0:1, :]) / z, 0.0)
        n_ref[h] = nrow
        b_ref[h] = jnp.exp(s2 - sv2[0:1, :])
        r_ref[h] = r2
        return 0

    lax.fori_loop(0, PEER_HEADS, head, 0)


def _peer_route(xn, wq, sk):
    n = xn.shape[0]
    t = SEQ_TILE
    fac = pl.BlockSpec((PEER_HEADS, N_KEYS, t), lambda i: (0, 0, i))
    shp = jax.ShapeDtypeStruct((PEER_HEADS, N_KEYS, n), F32)
    return pl.pallas_call(
        _peer_route_body,
        grid=(n // t,),
        in_specs=[
            pl.BlockSpec((t, D_MODEL), lambda i: (i, 0)),
            pl.BlockSpec(wq.shape, lambda i: (0, 0)),
            pl.BlockSpec(sk.shape, lambda i: (0, 0, 0)),
        ],
        out_specs=[fac, fac, fac, fac],
        out_shape=[shp, shp, shp, shp],
        scratch_shapes=[
            pltpu.VMEM((2 * PEER_HEADS, t, D_HALF), BF16),
            pltpu.VMEM((2, PEER_TOPK, t), F32),
        ],
        compiler_params=_cparams(("arbitrary",)),
        name="peer_route",
    )(xn, wq, sk)


def _peer_expert_body(xn_ref, a_ref, n_ref, b_ref, r_ref, u_ref, vt_ref, x1_ref,
                      o_ref, acc_ref, *, n_blocks):
    e = pl.program_id(1)

    @pl.when(e == 0)
    def _():
        acc_ref[...] = jnp.zeros_like(acc_ref)

    act = _dot_nt(u_ref[...], xn_ref[...])
    parts = []
    for i in range(EXP_ROWS):
        ge = jax.nn.gelu(act[i * N_KEYS:(i + 1) * N_KEYS, :], approximate=True)
        gate = jnp.zeros(ge.shape, F32)
        for h in range(PEER_HEADS):
            keep = r_ref[h] < n_ref[h, i:i + 1, :]
            gate = gate + jnp.where(keep, b_ref[h], 0.0) * a_ref[h, i:i + 1, :]
        parts.append((gate * ge).astype(BF16))
    w = jnp.concatenate(parts, axis=0)
    acc_ref[...] += _dot(vt_ref[...], w)

    @pl.when(e == n_blocks - 1)
    def _():
        o_ref[...] = x1_ref[...] + acc_ref[...].T


def _peer_expert(xn, fa, fn, fb, fr, u, vt, x1):
    n = xn.shape[0]
    t = TOK_TILE
    eb = EXP_ROWS * N_KEYS
    n_blocks = u.shape[0] // eb
    tok = lambda i, e: (i, 0)
    return pl.pallas_call(
        functools.partial(_peer_expert_body, n_blocks=n_blocks),
        grid=(n // t, n_blocks),
        in_specs=[
            pl.BlockSpec((t, D_MODEL), tok),
            pl.BlockSpec((PEER_HEADS, EXP_ROWS, t), lambda i, e: (0, e, i)),
            pl.BlockSpec((PEER_HEADS, EXP_ROWS, t), lambda i, e: (0, e, i)),
            pl.BlockSpec((PEER_HEADS, N_KEYS, t), lambda i, e: (0, 0, i)),
            pl.BlockSpec((PEER_HEADS, N_KEYS, t), lambda i, e: (0, 0, i)),
            pl.BlockSpec((eb, D_MODEL), lambda i, e: (e, 0)),
            pl.BlockSpec((D_MODEL, eb), lambda i, e: (0, e)),
            pl.BlockSpec((t, D_MODEL), tok),
        ],
        out_specs=pl.BlockSpec((t, D_MODEL), tok),
        out_shape=jax.ShapeDtypeStruct((n, D_MODEL), F32),
        scratch_shapes=[pltpu.VMEM((D_MODEL, t), F32)],
        compiler_params=_cparams(("arbitrary", "arbitrary")),
        name="peer_expert",
    )(xn, fa, fn, fb, fr, u, vt, x1)


def _selectors():
    import numpy as np
    wide = D_ATTN + 4 * LANES
    selq = np.zeros((N_HEADS, wide, AUG), np.float32)
    selk = np.zeros((N_HEADS, wide, AUG), np.float32)
    selv = np.zeros((N_HEADS, AUG, D_ATTN + LANES), np.float32)
    ones_row = D_ATTN + 3 * LANES
    for h in range(N_HEADS):
        for d in range(HEAD_DIM):
            selq[h, h * HEAD_DIM + d, d] = ATTN_SCALE
            selk[h, h * HEAD_DIM + d, d] = 1.0
            selv[h, d, h * HEAD_DIM + d] = 1.0
        for piece in range(3):
            selq[h, D_ATTN + piece * LANES + h, HEAD_DIM + piece] = 1.0
            selq[h, ones_row, HEAD_DIM + 3 + piece] = 1.0
            selk[h, ones_row, HEAD_DIM + piece] = 1.0
            selk[h, D_ATTN + piece * LANES + h, HEAD_DIM + 3 + piece] = -1.0
        selv[h, HEAD_DIM, D_ATTN] = 1.0
    return (jnp.asarray(selq, BF16), jnp.asarray(selk, BF16), jnp.asarray(selv, BF16))


def kernel(x_prompt, x_sample, cache_k, cache_v, cache_logf, state_pool, page_table,
           norm_mix, w_in, b_forget, q_norm, k_norm, pool_w, pool_scale, w_out,
           norm_ffn, peer_wq, peer_subkeys, peer_u, peer_v):
    import numpy as np
    batch, seq, _ = x_prompt.shape
    n_seq, dec_seq, _ = x_sample.shape
    depth = w_in.shape[0]
    n_pool = cache_k.shape[1]
    n_pages = page_table.shape[1]
    past = n_pages * PAGE_SIZE
    assert dec_seq == 1 and seq % KV_TILE == 0 and n_pages % PAGES_PER_STEP == 0
    assert n_seq % 8 == 0
    n_prompt = batch * seq
    n_tok = n_prompt + n_seq
    n_all = -(-n_tok // TOK_TILE) * TOK_TILE

    x = jnp.concatenate([x_prompt.reshape(n_prompt, D_MODEL), x_sample.reshape(n_seq, D_MODEL),
                         jnp.zeros((n_all - n_tok, D_MODEL), F32)], axis=0)
    selq, selk, selv = _selectors()
    hm = jnp.asarray(np.kron(np.eye(N_HEADS), np.full((HEAD_DIM, HEAD_DIM), 1.0 / HEAD_DIM)), BF16)
    jj = np.arange(PAGE_SIZE)
    sfx = jnp.asarray(np.concatenate([(jj[:, None] > jj[None, :]).astype(np.float32),
                                      np.ones((PAGE_SIZE, PAGE_SIZE), np.float32)], axis=1), BF16)
    ck = cache_k.reshape(depth, n_pool, PAGE_SIZE, D_ATTN)
    cv = cache_v.reshape(depth, n_pool, PAGE_SIZE, D_ATTN)

    kp_l, vp_l, fp_l, pp_l, ks_l, vs_l, fs_l, ps_l = [], [], [], [], [], [], [], []
    for l in range(depth):
        wm = w_in[l][:, :D_POOL + 3 * D_ATTN].astype(BF16)
        wf = jnp.pad(w_in[l][:, D_POOL + 3 * D_ATTN:], ((0, 0), (0, LANES - N_HEADS))).astype(BF16)
        bfp = jnp.pad(b_forget[l], (0, LANES - N_HEADS)).reshape(1, LANES)
        xp, q, k, v, lf = _inproj(x, norm_mix[l].reshape(1, D_MODEL), wm, wf, bfp,
                                  jnp.tile(q_norm[l], N_HEADS).reshape(1, D_ATTN),
                                  jnp.tile(k_norm[l], N_HEADS).reshape(1, D_ATTN), hm)
        pw = pool_w[l].astype(BF16)
        ps = pool_scale[l].reshape(1, D_POOL)

        po_p, qa, ka, vt = _poolprep(xp, lf, q, k, v, pw, ps, selq, selk, selv, batch, seq)
        at_p = _flash(qa, ka, vt, batch, seq)

        sl = slice(n_prompt, n_tok)
        xp_s = xp[sl]
        po_s = _sample_pool(xp_s, jnp.transpose(state_pool[l], (1, 0, 2)), pw, ps, past)
        at_s = _sample_attn(page_table, q[sl].reshape(n_seq, 1, D_ATTN), k[sl].reshape(n_seq, 1, D_ATTN),
                            v[sl].reshape(n_seq, 1, D_ATTN), lf[sl].reshape(n_seq, 1, LANES), sfx,
                            ck, cv, cache_logf, l).reshape(n_seq, D_ATTN)

        pad = jnp.zeros((n_all - n_tok, D_POOL), F32)
        po = jnp.concatenate([po_p, po_s, pad], axis=0)
        at = jnp.concatenate([at_p, at_s, pad], axis=0)
        x1, xn = _outproj(x, po, at, w_out[l].astype(BF16), norm_ffn[l].reshape(1, D_MODEL))
        fa, fn, fb, fr = _peer_route(xn, peer_wq[l].astype(BF16), peer_subkeys[l].astype(BF16))
        x = _peer_expert(xn, fa, fn, fb, fr, peer_u[l].astype(BF16),
                         jnp.transpose(peer_v[l]).astype(BF16), x1)

        kp_l.append(k[:n_prompt].reshape(batch, seq, N_HEADS, HEAD_DIM))
        vp_l.append(v[:n_prompt].reshape(batch, seq, N_HEADS, HEAD_DIM))
        fp_l.append(lf[:n_prompt, :N_HEADS].reshape(batch, seq, N_HEADS))
        pp_l.append(xp[:n_prompt].reshape(batch, seq, D_POOL)[:, seq - POOL_STATE:])
        ks_l.append(k[sl].reshape(n_seq, 1, N_HEADS, HEAD_DIM))
        vs_l.append(v[sl].reshape(n_seq, 1, N_HEADS, HEAD_DIM))
        fs_l.append(lf[sl, :N_HEADS].reshape(n_seq, 1, N_HEADS))
        ps_l.append(jnp.concatenate([state_pool[l][:, 1:], xp_s[:, None, :]], axis=1))

    return (x[:n_prompt].reshape(batch, seq, D_MODEL), x[sl].reshape(n_seq, 1, D_MODEL),
            jnp.stack(kp_l), jnp.stack(vp_l), jnp.stack(fp_l), jnp.stack(pp_l),
            jnp.stack(ks_l), jnp.stack(vs_l), jnp.stack(fs_l), jnp.stack(ps_l))
```

```python
import functools

import jax
import jax.numpy as jnp
import numpy as np
from jax import lax
from jax.experimental import pallas as pl
from jax.experimental.pallas import tpu as pltpu

F32 = jnp.float32
BF16 = jnp.bfloat16

EPS = 1e-6
D_MODEL = 1024
D_POOL = 512
POOL_WINDOWS = (2, 4, 8, 16)
POOL_GROUP = 128
POOL_STATE = 15
N_HEADS = 8
HEAD_DIM = 64
D_ATTN = N_HEADS * HEAD_DIM
PAGE_SIZE = 128
PEER_HEADS = 8
PEER_TOPK = 16
N_KEYS = 128
D_HALF = 128
ATTN_SCALE = HEAD_DIM ** -0.5

LANES = 128
AUG = 128
V_ROWS = 80
TOK_TILE = 512
SEQ_TILE = 256
Q_TILE = 256
KV_TILE = 512
EXP_ROWS = 8
SUB_ROWS = 8
PAGES_PER_STEP = 8
VMEM_LIMIT = 56 * 1024 * 1024
NEG = -1e30


def _cparams(sem):
    return pltpu.CompilerParams(dimension_semantics=sem, vmem_limit_bytes=VMEM_LIMIT)


def _split3(x):
    hi = x.astype(BF16)
    r = x - hi.astype(F32)
    mid = r.astype(BF16)
    lo = (r - mid.astype(F32)).astype(BF16)
    return hi, mid, lo


def _dot(a, b):
    return jnp.dot(a, b, preferred_element_type=F32)


def _dot_nt(a, b):
    return lax.dot_general(a, b, (((1,), (1,)), ((), ())), preferred_element_type=F32)


def _inproj_body(x_ref, g_ref, wm_ref, wf_ref, bf_ref, qg_ref, kg_ref, hm_ref,
                 xp_ref, q_ref, k_ref, v_ref, lf_ref):
    x = x_ref[...]
    ms = jnp.mean(x * x, axis=-1, keepdims=True)
    h = (x * lax.rsqrt(ms + EPS) * g_ref[...]).astype(BF16)
    z = _dot(h, wm_ref[...])
    xp_ref[...] = z[:, 0:D_POOL]
    hm = hm_ref[...]

    def headnorm(t, gain):
        sq = t * t
        hi = sq.astype(BF16)
        lo = (sq - hi.astype(F32)).astype(BF16)
        msh = _dot(hi, hm) + _dot(lo, hm)
        return t * lax.rsqrt(msh + EPS) * gain

    q_ref[...] = headnorm(z[:, D_POOL:D_POOL + D_ATTN], qg_ref[...])
    k_ref[...] = headnorm(z[:, D_POOL + D_ATTN:D_POOL + 2 * D_ATTN], kg_ref[...])
    v_ref[...] = z[:, D_POOL + 2 * D_ATTN:D_POOL + 3 * D_ATTN]
    f = _dot(h, wf_ref[...]) + bf_ref[...]
    lf_ref[...] = jnp.minimum(f, 0.0) - jnp.log1p(jnp.exp(-jnp.abs(f)))


def _inproj(x, g, wm, wf, bfp, qg, kg, hm):
    n = x.shape[0]
    t = TOK_TILE
    row = lambda i: (i, 0)
    fix = lambda i: (0, 0)
    return pl.pallas_call(
        _inproj_body,
        grid=(n // t,),
        in_specs=[
            pl.BlockSpec((t, D_MODEL), row),
            pl.BlockSpec((1, D_MODEL), fix),
            pl.BlockSpec(wm.shape, fix),
            pl.BlockSpec(wf.shape, fix),
            pl.BlockSpec((1, LANES), fix),
            pl.BlockSpec((1, D_ATTN), fix),
            pl.BlockSpec((1, D_ATTN), fix),
            pl.BlockSpec(hm.shape, fix),
        ],
        out_specs=[
            pl.BlockSpec((t, D_POOL), row),
            pl.BlockSpec((t, D_ATTN), row),
            pl.BlockSpec((t, D_ATTN), row),
            pl.BlockSpec((t, D_ATTN), row),
            pl.BlockSpec((t, LANES), row),
        ],
        out_shape=[
            jax.ShapeDtypeStruct((n, D_POOL), F32),
            jax.ShapeDtypeStruct((n, D_ATTN), F32),
            jax.ShapeDtypeStruct((n, D_ATTN), F32),
            jax.ShapeDtypeStruct((n, D_ATTN), F32),
            jax.ShapeDtypeStruct((n, LANES), F32),
        ],
        compiler_params=_cparams(("arbitrary",)),
        name="inproj",
    )(x, g, wm, wf, bfp, qg, kg, hm)


def _pool_mix(xp, window_sum, cnt_fn, pw_ref, ps_ref):
    outs = []
    for g, w in enumerate(POOL_WINDOWS):
        lanes = slice(g * POOL_GROUP, (g + 1) * POOL_GROUP)
        pooled = window_sum(g, w) / cnt_fn(w) - xp[:, lanes]
        outs.append(_dot(pooled.astype(BF16), pw_ref[g]))
    return jnp.concatenate(outs, axis=1) * ps_ref[...]


def _poolprep_body(xp_ref, lf_ref, q_ref, k_ref, v_ref, pw_ref, ps_ref,
                   selq_ref, selk_ref, selv_ref,
                   po_ref, qa_ref, ka_ref, vt_ref, xx_ref, fc_ref):
    t = SEQ_TILE
    hist = 16
    step = pl.program_id(1)

    @pl.when(step == 0)
    def _():
        xx_ref[0:hist, :] = jnp.zeros((hist, D_POOL), F32)
        fc_ref[...] = jnp.zeros_like(fc_ref)

    xp = xp_ref[...]
    xx_ref[hist:hist + t, :] = xp
    pos = lax.broadcasted_iota(jnp.int32, (t, POOL_GROUP), 0) + step * t

    def window_sum(g, w):
        lanes = slice(g * POOL_GROUP, (g + 1) * POOL_GROUP)
        ws = xp[:, lanes]
        for r in range(1, w):
            ws = ws + xx_ref[hist - r:hist - r + t, lanes]
        return ws

    def cnt(w):
        return jnp.minimum(pos + 1, w).astype(F32)

    po_ref[...] = _pool_mix(xp, window_sum, cnt, pw_ref, ps_ref)
    xx_ref[0:hist, :] = xx_ref[t:t + hist, :]

    ri = lax.broadcasted_iota(jnp.int32, (t, t), 0)
    ci = lax.broadcasted_iota(jnp.int32, (t, t), 1)
    tri = jnp.where(ci <= ri, 1.0, 0.0).astype(BF16)
    hi, mid, lo = _split3(lf_ref[...])
    fcum = _dot(tri, hi) + _dot(tri, mid) + _dot(tri, lo) + fc_ref[...]
    fc_ref[...] = fcum[t - 1:t, :]
    fh, fm, fl = _split3(fcum)
    ones = jnp.ones((t, LANES), BF16)
    wq = jnp.concatenate([q_ref[...].astype(BF16), fh, fm, fl, ones], axis=1)
    wk = jnp.concatenate([k_ref[...].astype(BF16), fh, fm, fl, ones], axis=1)
    wv = jnp.concatenate([v_ref[...].astype(BF16), ones], axis=1)
    for h in range(N_HEADS):
        qa_ref[h] = _dot(wq, selq_ref[h]).astype(BF16)
        ka_ref[h] = _dot(wk, selk_ref[h]).astype(BF16)
        vt_ref[h] = _dot_nt(selv_ref[h], wv).astype(BF16)


def _poolprep(xp, lf, q, k, v, pw, ps, selq, selk, selv, batch, seq):
    t = SEQ_TILE
    nt = seq // t
    n = batch * seq
    row = lambda b, i: (b * nt + i, 0)
    fix2 = lambda b, i: (0, 0)
    fix3 = lambda b, i: (0, 0, 0)
    return pl.pallas_call(
        _poolprep_body,
        grid=(batch, nt),
        in_specs=[
            pl.BlockSpec((t, D_POOL), row),
            pl.BlockSpec((t, LANES), row),
            pl.BlockSpec((t, D_ATTN), row),
            pl.BlockSpec((t, D_ATTN), row),
            pl.BlockSpec((t, D_ATTN), row),
            pl.BlockSpec(pw.shape, fix3),
            pl.BlockSpec((1, D_POOL), fix2),
            pl.BlockSpec(selq.shape, fix3),
            pl.BlockSpec(selk.shape, fix3),
            pl.BlockSpec(selv.shape, fix3),
        ],
        out_specs=[
            pl.BlockSpec((t, D_POOL), row),
            pl.BlockSpec((N_HEADS, t, AUG), lambda b, i: (0, b * nt + i, 0)),
            pl.BlockSpec((N_HEADS, t, AUG), lambda b, i: (0, b * nt + i, 0)),
            pl.BlockSpec((N_HEADS, AUG, t), lambda b, i: (0, 0, b * nt + i)),
        ],
        out_shape=[
            jax.ShapeDtypeStruct((n, D_POOL), F32),
            jax.ShapeDtypeStruct((N_HEADS, n, AUG), BF16),
            jax.ShapeDtypeStruct((N_HEADS, n, AUG), BF16),
            jax.ShapeDtypeStruct((N_HEADS, AUG, n), BF16),
        ],
        scratch_shapes=[
            pltpu.VMEM((t + 16, D_POOL), F32),
            pltpu.VMEM((1, LANES), F32),
        ],
        compiler_params=_cparams(("arbitrary", "arbitrary")),
        name="poolprep",
    )(xp, lf, q, k, v, pw, ps, selq, selk, selv)


def _flash_body(qa_ref, ka_ref, vt_ref, o_ref, s0_scr, s1_scr):
    tq, tk = Q_TILE, KV_TILE
    i = pl.program_id(2)
    n_full = (i * tq) // tk
    qpos = lax.broadcasted_iota(jnp.int32, (tk, tq), 1) + i * tq
    krel = lax.broadcasted_iota(jnp.int32, (tk, tq), 0)
    vrows = V_ROWS

    s_slots = (s0_scr, s1_scr)

    def scores(j, slot):
        start = pl.multiple_of(j * tk, tk)
        for hh in range(2):
            s_slots[slot][hh] = _dot_nt(ka_ref[hh, pl.ds(start, tk), :], qa_ref[hh])

    def tile(j, slot, carry, masked):
        if not masked:
            scores(j + 1, 1 - slot)
        start = pl.multiple_of(j * tk, tk)
        new = []
        for hh in range(2):
            m, acc = carry[hh]
            s = s_slots[slot][hh]
            if masked:
                s = jnp.where(krel + j * tk <= qpos, s, NEG)
            m_new = jnp.maximum(m, jnp.max(s, axis=0, keepdims=True))
            p = jnp.exp(s - m_new).astype(BF16)
            alpha = jnp.exp(m - m_new)
            acc = alpha * acc + _dot(vt_ref[hh, 0:vrows, pl.ds(start, tk)], p)
            new.append((m_new, acc))
        return tuple(new)

    def pair(jj, carry):
        carry = tile(2 * jj, 0, carry, False)
        return tile(2 * jj + 1, 1, carry, False)

    scores(0, 0)
    one = (jnp.full((1, tq), NEG, F32), jnp.zeros((vrows, tq), F32))
    carry = lax.fori_loop(0, n_full // 2, pair, (one, one))
    j0 = 2 * (n_full // 2)
    carry = lax.cond(
        n_full % 2 == 1,
        lambda c: tile(j0 + 1, 1, tile(j0, 0, c, False), True),
        lambda c: tile(j0, 0, c, True),
        carry)
    outs = []
    for hh in range(2):
        acc = jnp.concatenate([carry[hh][1], jnp.zeros((AUG - vrows, tq), F32)], axis=0)
        acc_t = acc.T
        outs.append(acc_t[:, 0:HEAD_DIM] / acc_t[:, HEAD_DIM:HEAD_DIM + 1])
    o_ref[...] = jnp.concatenate(outs, axis=1)


def _flash(qa, ka, vt, batch, seq):
    tq = Q_TILE
    nq = seq // tq
    n = batch * seq
    return pl.pallas_call(
        _flash_body,
        grid=(batch, N_HEADS // 2, nq),
        in_specs=[
            pl.BlockSpec((2, tq, AUG), lambda b, hp, i: (hp, b * nq + i, 0)),
            pl.BlockSpec((2, seq, AUG), lambda b, hp, i: (hp, b, 0)),
            pl.BlockSpec((2, AUG, seq), lambda b, hp, i: (hp, 0, b)),
        ],
        out_specs=pl.BlockSpec((tq, 2 * HEAD_DIM), lambda b, hp, i: (b * nq + i, hp)),
        out_shape=jax.ShapeDtypeStruct((n, D_ATTN), F32),
        scratch_shapes=[pltpu.VMEM((2, KV_TILE, Q_TILE), F32), pltpu.VMEM((2, KV_TILE, Q_TILE), F32)],
        compiler_params=_cparams(("arbitrary", "arbitrary", "arbitrary")),
        name="flash",
    )(qa, ka, vt)


def _sample_pool_body(xp_ref, st_ref, pw_ref, ps_ref, o_ref, *, start):
    xp = xp_ref[...]

    def window_sum(g, w):
        lanes = slice(g * POOL_GROUP, (g + 1) * POOL_GROUP)
        ws = xp[:, lanes]
        for r in range(1, w):
            ws = ws + st_ref[POOL_STATE - r][:, lanes]
        return ws

    o_ref[...] = _pool_mix(xp, window_sum, lambda w: float(min(start + 1, w)), pw_ref, ps_ref)


def _sample_pool(xp_s, state_t, pw, ps, start):
    return pl.pallas_call(
        functools.partial(_sample_pool_body, start=start),
        out_shape=jax.ShapeDtypeStruct(xp_s.shape, F32),
        compiler_params=pltpu.CompilerParams(vmem_limit_bytes=VMEM_LIMIT),
        name="sample_pool",
    )(xp_s, state_t, pw, ps)


def _residue_reduce(v, op):
    shift = N_HEADS
    while shift < LANES:
        v = op(v, pltpu.roll(v, shift, axis=1))
        shift *= 2
    return v


def _sample_attn_body(pt_ref, q_ref, k_ref, v_ref, lf_ref, lmat_ref, qmat_ref, *refs, n_steps):
    g_pages = PAGES_PER_STEP
    ck = refs[0:g_pages]
    cv = refs[g_pages:2 * g_pages]
    cl = refs[2 * g_pages:3 * g_pages]
    o_ref = refs[3 * g_pages]
    m_ref, l_ref, acc_ref, car_ref = refs[3 * g_pages + 1:]
    rows = 2 * N_HEADS
    flat = PAGE_SIZE * N_HEADS
    per = flat // LANES
    step = pl.program_id(1)

    lane = lax.broadcasted_iota(jnp.int32, (rows, LANES), 1)
    sub = lax.broadcasted_iota(jnp.int32, (rows, LANES), 0)
    diag = (lane & (N_HEADS - 1)) == sub
    eye = lane == sub

    def to_col(v):
        full = jnp.concatenate([v, v], axis=0)
        return jnp.sum(jnp.where(eye, full, 0.0), axis=1, keepdims=True)

    qh = q_ref[0].astype(BF16).astype(F32) * ATTN_SCALE
    qh16 = jnp.concatenate([qh, jnp.zeros_like(qh)], axis=0).astype(BF16)

    @pl.when(step == 0)
    def _():
        m_ref[...] = jnp.full(m_ref.shape, NEG, F32)
        l_ref[...] = jnp.zeros_like(l_ref)
        acc_ref[...] = jnp.zeros_like(acc_ref)
        car_ref[...] = jnp.broadcast_to(lf_ref[0], car_ref.shape)

    x_all = jnp.concatenate([cl[g][...] for g in range(g_pages)], axis=0)
    hi, mid, lo = _split3(x_all)
    lmat = lmat_ref[...]
    qmat = qmat_ref[...]
    in_row = _dot(hi, lmat) + _dot(mid, lmat) + _dot(lo, lmat)
    row_tot = _dot(hi, qmat) + _dot(mid, qmat) + _dot(lo, qmat)
    nr = g_pages * per
    later = jnp.where(lax.broadcasted_iota(jnp.int32, (nr, nr), 1)
                      > lax.broadcasted_iota(jnp.int32, (nr, nr), 0), 1.0, 0.0).astype(BF16)
    th, tm, tl = _split3(row_tot)
    carry = car_ref[...]
    bias = in_row + _dot(later, th) + _dot(later, tm) + _dot(later, tl) + carry[0:1, :]
    car_ref[...] = carry + jnp.sum(row_tot, axis=0, keepdims=True)

    s_rows = []
    for g in range(g_pages):
        k2 = ck[g][...].reshape(flat, HEAD_DIM).astype(BF16)
        st = _dot_nt(qh16, k2)
        for a in range(per):
            blk = st[:, a * LANES:(a + 1) * LANES]
            s_rows.append(jnp.sum(jnp.where(diag, blk, 0.0), axis=0, keepdims=True))
    s_all = jnp.concatenate(s_rows, axis=0) + bias

    m_old = m_ref[...]
    m_step = jnp.broadcast_to(jnp.max(s_all, axis=0, keepdims=True), m_old.shape)
    m_new = jnp.maximum(m_old, _residue_reduce(m_step, jnp.maximum))
    p = jnp.exp(s_all - m_new[0:1, :])
    alpha = jnp.exp(m_old - m_new)
    l_step = jnp.broadcast_to(jnp.sum(p, axis=0, keepdims=True), m_old.shape)
    l_ref[...] = alpha * l_ref[...] + _residue_reduce(l_step, jnp.add)
    m_ref[...] = m_new

    pv = jnp.zeros(acc_ref.shape, F32)
    for g in range(g_pages):
        pg = jnp.concatenate(
            [jnp.where(diag, jnp.broadcast_to(p[g * per + a:g * per + a + 1, :], (rows, LANES)), 0.0)
             for a in range(per)], axis=1).astype(BF16)
        v2 = cv[g][...].reshape(flat, HEAD_DIM).astype(BF16)
        pv = pv + _dot(pg, v2)
    acc_ref[...] = to_col(alpha) * acc_ref[...] + pv

    @pl.when(step == n_steps - 1)
    def _():
        kf = k_ref[0].astype(BF16).astype(F32)
        vf = v_ref[0].astype(BF16).astype(F32)
        s_new = jnp.sum(qh * kf, axis=1, keepdims=True)
        m_col = to_col(m_ref[...])[0:N_HEADS]
        l_col = to_col(l_ref[...])[0:N_HEADS]
        m_fin = jnp.maximum(m_col, s_new)
        a = jnp.exp(m_col - m_fin)
        p_new = jnp.exp(s_new - m_fin)
        l_fin = a * l_col + p_new
        acc = a * acc_ref[0:N_HEADS, :] + p_new.astype(BF16).astype(F32) * vf
        o_ref[0] = acc / l_fin


def _sample_attn(page_table, q_s, k_s, v_s, lf_s, lmat, qmat, cache_k, cache_v, cache_lf, layer):
    n_seq, n_pages = page_table.shape
    g_pages = PAGES_PER_STEP
    n_steps = n_pages // g_pages
    rows = 2 * N_HEADS
    per = PAGE_SIZE * N_HEADS // LANES

    def page_map(g, nd):
        def index(b, s, pt):
            return (layer, pt[b, n_pages - (s + 1) * g_pages + g]) + (0,) * nd
        return index

    tok = lambda b, s, pt: (b, 0, 0)
    fix = lambda b, s, pt: (0, 0)
    in_specs = [
        pl.BlockSpec((1, N_HEADS, HEAD_DIM), tok),
        pl.BlockSpec((1, N_HEADS, HEAD_DIM), tok),
        pl.BlockSpec((1, N_HEADS, HEAD_DIM), tok),
        pl.BlockSpec((1, 1, LANES), tok),
        pl.BlockSpec(lmat.shape, fix),
        pl.BlockSpec(qmat.shape, fix),
    ]
    page = (None, None, PAGE_SIZE, N_HEADS, HEAD_DIM)
    in_specs += [pl.BlockSpec(page, page_map(g, 3)) for g in range(g_pages)]
    in_specs += [pl.BlockSpec(page, page_map(g, 3)) for g in range(g_pages)]
    in_specs += [pl.BlockSpec((None, None, per, LANES), page_map(g, 2)) for g in range(g_pages)]
    grid_spec = pltpu.PrefetchScalarGridSpec(
        num_scalar_prefetch=1,
        grid=(n_seq, n_steps),
        in_specs=in_specs,
        out_specs=pl.BlockSpec((1, N_HEADS, HEAD_DIM), tok),
        scratch_shapes=[
            pltpu.VMEM((N_HEADS, LANES), F32),
            pltpu.VMEM((N_HEADS, LANES), F32),
            pltpu.VMEM((rows, HEAD_DIM), F32),
            pltpu.VMEM((N_HEADS, LANES), F32),
        ],
    )
    return pl.pallas_call(
        functools.partial(_sample_attn_body, n_steps=n_steps),
        grid_spec=grid_spec,
        out_shape=jax.ShapeDtypeStruct((n_seq, N_HEADS, HEAD_DIM), F32),
        compiler_params=_cparams(("arbitrary", "arbitrary")),
        name="sample_attn",
    )(page_table, q_s, k_s, v_s, lf_s, lmat, qmat,
      *([cache_k] * g_pages), *([cache_v] * g_pages), *([cache_lf] * g_pages))


def _outproj_body(x_ref, po_ref, at_ref, wo_ref, g_ref, x1_ref, xn_ref):
    mix = jnp.concatenate([po_ref[...], at_ref[...]], axis=1).astype(BF16)
    x1 = x_ref[...] + _dot(mix, wo_ref[...])
    x1_ref[...] = x1
    ms = jnp.mean(x1 * x1, axis=-1, keepdims=True)
    xn_ref[...] = (x1 * lax.rsqrt(ms + EPS) * g_ref[...]).astype(BF16)


def _outproj(x, po, at, wo, g):
    n = x.shape[0]
    t = TOK_TILE
    row = lambda i: (i, 0)
    fix = lambda i: (0, 0)
    return pl.pallas_call(
        _outproj_body,
        grid=(n // t,),
        in_specs=[
            pl.BlockSpec((t, D_MODEL), row),
            pl.BlockSpec((t, D_POOL), row),
            pl.BlockSpec((t, D_ATTN), row),
            pl.BlockSpec(wo.shape, fix),
            pl.BlockSpec((1, D_MODEL), fix),
        ],
        out_specs=[pl.BlockSpec((t, D_MODEL), row), pl.BlockSpec((t, D_MODEL), row)],
        out_shape=[jax.ShapeDtypeStruct((n, D_MODEL), F32), jax.ShapeDtypeStruct((n, D_MODEL), BF16)],
        compiler_params=_cparams(("arbitrary",)),
        name="outproj",
    )(x, po, at, wo, g)


def _top16_ranks(s, sv_ref, half, rowi):
    rank = jnp.full(s.shape, float(PEER_TOPK), F32)
    for c in range(PEER_TOPK):
        m = jnp.max(s, axis=0, keepdims=True)
        first = jnp.min(jnp.where(s == m, rowi, float(N_KEYS)), axis=0, keepdims=True)
        sel = rowi == first
        rank = jnp.where(sel, float(c), rank)
        s = jnp.where(sel, -jnp.inf, s)
        sv_ref[half, c:c + 1, :] = m
    return rank


def _top16_ranks_distinct(s, sv_ref, half):
    rank = jnp.full(s.shape, float(PEER_TOPK), F32)
    for c in range(PEER_TOPK):
        m = jnp.max(s, axis=0, keepdims=True)
        sel = s == m
        rank = jnp.where(sel, float(c), rank)
        s = jnp.where(sel, -jnp.inf, s)
        sv_ref[half, c:c + 1, :] = m
    return rank


def _pair_segments():
    segs, row = [], 0
    for c in range(PEER_TOPK):
        nd = PEER_TOPK // (c + 1)
        if nd < 8:
            break
        segs.append((row, c, nd))
        row += nd
    bins = []
    for c in range(len(segs), PEER_TOPK):
        nd = PEER_TOPK // (c + 1)
        for bn in bins:
            if bn[0] + nd <= 8:
                bn[1].append((bn[0], c, nd))
                bn[0] += nd
                break
        else:
            bins.append([nd, [(0, c, nd)]])
    for bn in bins:
        segs += [(row + o, c, nd) for o, c, nd in bn[1]]
        row += 8
    return segs, row


def _peer_route_body(xn_ref, wq_ref, sk_ref, flat_ref, a_ref, n_ref, b_ref, r_ref,
                     q_scr, sv_ref, rk_ref, cand_scr, wgt_scr):
    t = xn_ref.shape[0]
    q = _dot(xn_ref[...], wq_ref[...]).astype(BF16)
    for hp in range(2 * PEER_HEADS):
        q_scr[hp] = q[:, hp * D_HALF:(hp + 1) * D_HALF]
    rowi = lax.broadcasted_iota(jnp.int32, (N_KEYS, t), 0).astype(F32)
    segs, n_rows = _pair_segments()
    flat = flat_ref[...]
    big = float(PEER_TOPK * PEER_TOPK)
    cand_scr[...] = jnp.full(cand_scr.shape, -jnp.inf, F32)
    wgt_scr[...] = jnp.zeros_like(wgt_scr)

    def head(h, _):
        s1 = _dot_nt(sk_ref[0], q_scr[2 * h])
        s2 = _dot_nt(sk_ref[1], q_scr[2 * h + 1])
        rk_ref[0] = _top16_ranks_distinct(s1, sv_ref, 0)
        rk_ref[1] = _top16_ranks_distinct(s2, sv_ref, 1)
        ranked = (jnp.sum(jnp.where(rk_ref[0] < float(PEER_TOPK), 1.0, 0.0), axis=0, keepdims=True)
                  + jnp.sum(jnp.where(rk_ref[1] < float(PEER_TOPK), 1.0, 0.0), axis=0, keepdims=True))

        @pl.when(jnp.max(jnp.abs(ranked - float(2 * PEER_TOPK))) > 0.0)
        def _():
            rk_ref[0] = _top16_ranks(s1, sv_ref, 0, rowi)
            rk_ref[1] = _top16_ranks(s2, sv_ref, 1, rowi)

        r1 = rk_ref[0]
        r2 = rk_ref[1]
        sv1 = sv_ref[0]
        sv2 = sv_ref[1]
        e1 = jnp.exp(sv1 - sv1[0:1, :])
        e2 = jnp.exp(sv2 - sv2[0:1, :])
        for row, c, nd in segs:
            cand_scr[row:row + nd, :] = sv1[c:c + 1, :] + sv2[0:nd, :]
            wgt_scr[row:row + nd, :] = e1[c:c + 1, :] * e2[0:nd, :]
        cand = cand_scr[...]
        chosen = jnp.zeros(cand.shape, F32)
        for _k in range(PEER_TOPK):
            m = jnp.max(cand, axis=0, keepdims=True)
            first = jnp.min(jnp.where(cand == m, flat, big), axis=0, keepdims=True)
            sel = flat == first
            chosen = jnp.where(sel, 1.0, chosen)
            cand = jnp.where(sel, -jnp.inf, cand)
        z = jnp.sum(chosen * wgt_scr[...], axis=0, keepdims=True)
        nrow = jnp.zeros((N_KEYS, t), F32)
        for row, c, nd in segs:
            n_c = jnp.sum(chosen[row:row + nd, :], axis=0, keepdims=True)
            nrow = jnp.where(r1 == float(c), n_c, nrow)
        a_ref[h] = jnp.where(r1 < float(PEER_TOPK), jnp.exp(s1 - sv1[0:1, :]) / z, 0.0)
        n_ref[h] = nrow
        b_ref[h] = jnp.exp(s2 - sv2[0:1, :]).astype(BF16)
        r_ref[h] = r2.astype(BF16)
        return 0

    lax.fori_loop(0, PEER_HEADS, head, 0)


def _peer_route(xn, wq, sk):
    n = xn.shape[0]
    t = SEQ_TILE
    fac = pl.BlockSpec((PEER_HEADS, N_KEYS, t), lambda i: (0, 0, i))
    shp = jax.ShapeDtypeStruct((PEER_HEADS, N_KEYS, n), F32)
    shp_bf = jax.ShapeDtypeStruct((PEER_HEADS, N_KEYS, n), BF16)
    segs, n_rows = _pair_segments()
    flat = np.full((n_rows, t), 1e9, np.float32)
    for row, c, nd in segs:
        flat[row:row + nd, :] = (c * PEER_TOPK + np.arange(nd, dtype=np.float32))[:, None]
    return pl.pallas_call(
        _peer_route_body,
        grid=(n // t,),
        in_specs=[
            pl.BlockSpec((t, D_MODEL), lambda i: (i, 0)),
            pl.BlockSpec(wq.shape, lambda i: (0, 0)),
            pl.BlockSpec(sk.shape, lambda i: (0, 0, 0)),
            pl.BlockSpec(flat.shape, lambda i: (0, 0)),
        ],
        out_specs=[fac, fac, fac, fac],
        out_shape=[shp, shp, shp_bf, shp_bf],
        scratch_shapes=[
            pltpu.VMEM((2 * PEER_HEADS, t, D_HALF), BF16),
            pltpu.VMEM((2, PEER_TOPK, t), F32),
            pltpu.VMEM((2, N_KEYS, t), F32),
            pltpu.VMEM((n_rows, t), F32),
            pltpu.VMEM((n_rows, t), F32),
        ],
        compiler_params=_cparams(("arbitrary",)),
        name="peer_route",
    )(xn, wq, sk, jnp.asarray(flat))


def _peer_expert_body(xn_ref, a_ref, n_ref, b_ref, r_ref, u_ref, vt_ref, x1_ref,
                      o_ref, acc_ref, *, n_blocks):
    e = pl.program_id(1)

    @pl.when(e == 0)
    def _():
        acc_ref[...] = jnp.zeros_like(acc_ref)

    a_rows = a_ref[...].astype(BF16)
    n_rows = n_ref[...].astype(BF16)
    zero = jnp.zeros((), BF16)
    xn = xn_ref[...]
    se = SUB_ROWS * N_KEYS
    acts = [_dot_nt(u_ref[s * se:(s + 1) * se, :], xn) for s in range(EXP_ROWS // SUB_ROWS)]
    total = None
    for s, act in enumerate(acts):
        parts = []
        for k in range(SUB_ROWS):
            i = s * SUB_ROWS + k
            ge = jax.nn.gelu(act[k * N_KEYS:(k + 1) * N_KEYS, :], approximate=True).astype(BF16)
            gate = None
            for h in range(PEER_HEADS):
                keep = r_ref[h] < n_rows[h, i:i + 1, :]
                term = jnp.where(keep, b_ref[h], zero) * a_rows[h, i:i + 1, :]
                gate = term if gate is None else gate + term
            parts.append(gate * ge)
        w = jnp.concatenate(parts, axis=0)
        d = _dot(vt_ref[:, s * se:(s + 1) * se], w)
        total = d if total is None else total + d
    acc_ref[...] += total

    @pl.when(e == n_blocks - 1)
    def _():
        o_ref[...] = x1_ref[...] + acc_ref[...].T


def _peer_expert(xn, fa, fn, fb, fr, u, vt, x1):
    n = xn.shape[0]
    t = TOK_TILE
    eb = EXP_ROWS * N_KEYS
    n_blocks = u.shape[0] // eb
    tok = lambda i, e: (i, 0)
    return pl.pallas_call(
        functools.partial(_peer_expert_body, n_blocks=n_blocks),
        grid=(n // t, n_blocks),
        in_specs=[
            pl.BlockSpec((t, D_MODEL), tok),
            pl.BlockSpec((PEER_HEADS, EXP_ROWS, t), lambda i, e: (0, e, i)),
            pl.BlockSpec((PEER_HEADS, EXP_ROWS, t), lambda i, e: (0, e, i)),
            pl.BlockSpec((PEER_HEADS, N_KEYS, t), lambda i, e: (0, 0, i)),
            pl.BlockSpec((PEER_HEADS, N_KEYS, t), lambda i, e: (0, 0, i)),
            pl.BlockSpec((eb, D_MODEL), lambda i, e: (e, 0)),
            pl.BlockSpec((D_MODEL, eb), lambda i, e: (0, e)),
            pl.BlockSpec((t, D_MODEL), tok),
        ],
        out_specs=pl.BlockSpec((t, D_MODEL), tok),
        out_shape=jax.ShapeDtypeStruct((n, D_MODEL), F32),
        scratch_shapes=[pltpu.VMEM((D_MODEL, t), F32)],
        compiler_params=_cparams(("arbitrary", "arbitrary")),
        name="peer_expert",
    )(xn, fa, fn, fb, fr, u, vt, x1)


def _selectors():
    wide = D_ATTN + 4 * LANES
    selq = np.zeros((N_HEADS, wide, AUG), np.float32)
    selk = np.zeros((N_HEADS, wide, AUG), np.float32)
    selv = np.zeros((N_HEADS, AUG, D_ATTN + LANES), np.float32)
    ones_row = D_ATTN + 3 * LANES
    for h in range(N_HEADS):
        for d in range(HEAD_DIM):
            selq[h, h * HEAD_DIM + d, d] = ATTN_SCALE
            selk[h, h * HEAD_DIM + d, d] = 1.0
            selv[h, d, h * HEAD_DIM + d] = 1.0
        for piece in range(3):
            selq[h, D_ATTN + piece * LANES + h, HEAD_DIM + piece] = 1.0
            selq[h, ones_row, HEAD_DIM + 3 + piece] = 1.0
            selk[h, ones_row, HEAD_DIM + piece] = 1.0
            selk[h, D_ATTN + piece * LANES + h, HEAD_DIM + 3 + piece] = -1.0
        selv[h, HEAD_DIM, D_ATTN] = 1.0
    return (jnp.asarray(selq, BF16), jnp.asarray(selk, BF16), jnp.asarray(selv, BF16))


def kernel(x_prompt, x_sample, cache_k, cache_v, cache_logf, state_pool, page_table,
           norm_mix, w_in, b_forget, q_norm, k_norm, pool_w, pool_scale, w_out,
           norm_ffn, peer_wq, peer_subkeys, peer_u, peer_v):
    batch, seq, _ = x_prompt.shape
    n_seq, dec_seq, _ = x_sample.shape
    depth = w_in.shape[0]
    n_pool = cache_k.shape[1]
    n_pages = page_table.shape[1]
    past = n_pages * PAGE_SIZE
    assert dec_seq == 1 and seq % KV_TILE == 0 and n_pages % PAGES_PER_STEP == 0
    assert n_seq % 8 == 0
    n_prompt = batch * seq
    n_tok = n_prompt + n_seq
    n_all = -(-n_tok // TOK_TILE) * TOK_TILE

    x = jnp.concatenate([x_prompt.reshape(n_prompt, D_MODEL), x_sample.reshape(n_seq, D_MODEL),
                         jnp.zeros((n_all - n_tok, D_MODEL), F32)], axis=0)
    selq, selk, selv = _selectors()
    hm = jnp.asarray(np.kron(np.eye(N_HEADS), np.full((HEAD_DIM, HEAD_DIM), 1.0 / HEAD_DIM)), BF16)
    cc = np.arange(LANES)
    same_head = (cc[:, None] % N_HEADS) == (cc[None, :] % N_HEADS)
    lmat = jnp.asarray(same_head & (cc[:, None] // N_HEADS > cc[None, :] // N_HEADS), BF16)
    qmat = jnp.asarray(same_head, BF16)
    cache_lf = cache_logf.reshape(depth, n_pool, PAGE_SIZE * N_HEADS // LANES, LANES)

    kp_l, vp_l, fp_l, pp_l, ks_l, vs_l, fs_l, ps_l = [], [], [], [], [], [], [], []
    for l in range(depth):
        wm = w_in[l][:, :D_POOL + 3 * D_ATTN].astype(BF16)
        wf = jnp.pad(w_in[l][:, D_POOL + 3 * D_ATTN:], ((0, 0), (0, LANES - N_HEADS))).astype(BF16)
        bfp = jnp.pad(b_forget[l], (0, LANES - N_HEADS)).reshape(1, LANES)
        xp, q, k, v, lf = _inproj(x, norm_mix[l].reshape(1, D_MODEL), wm, wf, bfp,
                                  jnp.tile(q_norm[l], N_HEADS).reshape(1, D_ATTN),
                                  jnp.tile(k_norm[l], N_HEADS).reshape(1, D_ATTN), hm)
        pw = pool_w[l].astype(BF16)
        ps = pool_scale[l].reshape(1, D_POOL)

        po_p, qa, ka, vt = _poolprep(xp, lf, q, k, v, pw, ps, selq, selk, selv, batch, seq)
        at_p = _flash(qa, ka, vt, batch, seq)

        sl = slice(n_prompt, n_tok)
        xp_s = xp[sl]
        po_s = _sample_pool(xp_s, jnp.transpose(state_pool[l], (1, 0, 2)), pw, ps, past)
        heads = lambda t: t[sl].reshape(n_seq, N_HEADS, HEAD_DIM)
        lf_s = jnp.tile(lf[sl, :N_HEADS], (1, LANES // N_HEADS)).reshape(n_seq, 1, LANES)
        at_s = _sample_attn(page_table, heads(q), heads(k), heads(v), lf_s, lmat, qmat,
                            cache_k, cache_v, cache_lf, l).reshape(n_seq, D_ATTN)

        pad = jnp.zeros((n_all - n_tok, D_POOL), F32)
        po = jnp.concatenate([po_p, po_s, pad], axis=0)
        at = jnp.concatenate([at_p, at_s, pad], axis=0)
        x1, xn = _outproj(x, po, at, w_out[l].astype(BF16), norm_ffn[l].reshape(1, D_MODEL))
        fa, fn, fb, fr = _peer_route(xn, peer_wq[l].astype(BF16), peer_subkeys[l].astype(BF16))
        x = _peer_expert(xn, fa, fn, fb, fr, peer_u[l].astype(BF16),
                         jnp.transpose(peer_v[l]).astype(BF16), x1)

        kp_l.append(k[:n_prompt].reshape(batch, seq, N_HEADS, HEAD_DIM))
        vp_l.append(v[:n_prompt].reshape(batch, seq, N_HEADS, HEAD_DIM))
        fp_l.append(lf[:n_prompt, :N_HEADS].reshape(batch, seq, N_HEADS))
        pp_l.append(xp[:n_prompt].reshape(batch, seq, D_POOL)[:, seq - POOL_STATE:])
        ks_l.append(k[sl].reshape(n_seq, 1, N_HEADS, HEAD_DIM))
        vs_l.append(v[sl].reshape(n_seq, 1, N_HEADS, HEAD_DIM))
        fs_l.append(lf[sl, :N_HEADS].reshape(n_seq, 1, N_HEADS))
        ps_l.append(jnp.concatenate([state_pool[l][:, 1:], xp_s[:, None, :]], axis=1))

    return (x[:n_prompt].reshape(batch, seq, D_MODEL), x[sl].reshape(n_seq, 1, D_MODEL),
            jnp.stack(kp_l), jnp.stack(vp_l), jnp.stack(fp_l), jnp.stack(pp_l),
            jnp.stack(ks_l), jnp.stack(vs_l), jnp.stack(fs_l), jnp.stack(ps_l))
```

```python
import functools

import jax
import jax.numpy as jnp
import numpy as np
from jax import lax
from jax.experimental import pallas as pl
from jax.experimental.pallas import tpu as pltpu

F32 = jnp.float32
BF16 = jnp.bfloat16

EPS = 1e-6
D_MODEL = 1024
D_POOL = 512
POOL_WINDOWS = (2, 4, 8, 16)
POOL_GROUP = 128
POOL_STATE = 15
N_HEADS = 8
HEAD_DIM = 64
D_ATTN = N_HEADS * HEAD_DIM
PAGE_SIZE = 128
PEER_HEADS = 8
PEER_TOPK = 16
N_KEYS = 128
D_HALF = 128
ATTN_SCALE = HEAD_DIM ** -0.5

LANES = 128
AUG = 128
V_ROWS = 80
TOK_TILE = 512
SEQ_TILE = 256
Q_TILE = 256
KV_TILE = 512
EXP_ROWS = 8
SUB_ROWS = 8
PAGES_PER_STEP = 8
VMEM_LIMIT = 56 * 1024 * 1024
NEG = -1e30


def _cparams(sem):
    return pltpu.CompilerParams(dimension_semantics=sem, vmem_limit_bytes=VMEM_LIMIT)


def _split3(x):
    hi = x.astype(BF16)
    r = x - hi.astype(F32)
    mid = r.astype(BF16)
    lo = (r - mid.astype(F32)).astype(BF16)
    return hi, mid, lo


def _dot(a, b):
    return jnp.dot(a, b, preferred_element_type=F32)


def _dot_nt(a, b):
    return lax.dot_general(a, b, (((1,), (1,)), ((), ())), preferred_element_type=F32)


def _inproj_body(x_ref, g_ref, wm_ref, wf_ref, bf_ref, qg_ref, kg_ref, hm_ref,
                 xp_ref, q_ref, k_ref, v_ref, lf_ref):
    x = x_ref[...]
    ms = jnp.mean(x * x, axis=-1, keepdims=True)
    h = (x * lax.rsqrt(ms + EPS) * g_ref[...]).astype(BF16)
    z = _dot(h, wm_ref[...])
    xp_ref[...] = z[:, 0:D_POOL]
    hm = hm_ref[...]

    def headnorm(t, gain):
        sq = t * t
        hi = sq.astype(BF16)
        lo = (sq - hi.astype(F32)).astype(BF16)
        msh = _dot(hi, hm) + _dot(lo, hm)
        return t * lax.rsqrt(msh + EPS) * gain

    q_ref[...] = headnorm(z[:, D_POOL:D_POOL + D_ATTN], qg_ref[...])
    k_ref[...] = headnorm(z[:, D_POOL + D_ATTN:D_POOL + 2 * D_ATTN], kg_ref[...])
    v_ref[...] = z[:, D_POOL + 2 * D_ATTN:D_POOL + 3 * D_ATTN]
    f = _dot(h, wf_ref[...]) + bf_ref[...]
    lf_ref[...] = jnp.minimum(f, 0.0) - jnp.log1p(jnp.exp(-jnp.abs(f)))


def _inproj(x, g, wm, wf, bfp, qg, kg, hm):
    n = x.shape[0]
    t = TOK_TILE
    row = lambda i: (i, 0)
    fix = lambda i: (0, 0)
    return pl.pallas_call(
        _inproj_body,
        grid=(n // t,),
        in_specs=[
            pl.BlockSpec((t, D_MODEL), row),
            pl.BlockSpec((1, D_MODEL), fix),
            pl.BlockSpec(wm.shape, fix),
            pl.BlockSpec(wf.shape, fix),
            pl.BlockSpec((1, LANES), fix),
            pl.BlockSpec((1, D_ATTN), fix),
            pl.BlockSpec((1, D_ATTN), fix),
            pl.BlockSpec(hm.shape, fix),
        ],
        out_specs=[
            pl.BlockSpec((t, D_POOL), row),
            pl.BlockSpec((t, D_ATTN), row),
            pl.BlockSpec((t, D_ATTN), row),
            pl.BlockSpec((t, D_ATTN), row),
            pl.BlockSpec((t, LANES), row),
        ],
        out_shape=[
            jax.ShapeDtypeStruct((n, D_POOL), F32),
            jax.ShapeDtypeStruct((n, D_ATTN), F32),
            jax.ShapeDtypeStruct((n, D_ATTN), F32),
            jax.ShapeDtypeStruct((n, D_ATTN), F32),
            jax.ShapeDtypeStruct((n, LANES), F32),
        ],
        compiler_params=_cparams(("arbitrary",)),
        name="inproj",
    )(x, g, wm, wf, bfp, qg, kg, hm)


def _pool_mix(xp, window_sum, cnt_fn, pw_ref, ps_ref):
    outs = []
    for g, w in enumerate(POOL_WINDOWS):
        lanes = slice(g * POOL_GROUP, (g + 1) * POOL_GROUP)
        pooled = window_sum(g, w) / cnt_fn(w) - xp[:, lanes]
        outs.append(_dot(pooled.astype(BF16), pw_ref[g]))
    return jnp.concatenate(outs, axis=1) * ps_ref[...]


def _poolprep_body(xp_ref, lf_ref, q_ref, k_ref, v_ref, pw_ref, ps_ref,
                   selq_ref, selk_ref, selv_ref,
                   po_ref, qa_ref, ka_ref, vt_ref, xx_ref, fc_ref):
    t = SEQ_TILE
    hist = 16
    step = pl.program_id(1)

    @pl.when(step == 0)
    def _():
        xx_ref[0:hist, :] = jnp.zeros((hist, D_POOL), F32)
        fc_ref[...] = jnp.zeros_like(fc_ref)

    xp = xp_ref[...]
    xx_ref[hist:hist + t, :] = xp
    pos = lax.broadcasted_iota(jnp.int32, (t, POOL_GROUP), 0) + step * t

    def window_sum(g, w):
        lanes = slice(g * POOL_GROUP, (g + 1) * POOL_GROUP)
        ws = xp[:, lanes]
        for r in range(1, w):
            ws = ws + xx_ref[hist - r:hist - r + t, lanes]
        return ws

    def cnt(w):
        return jnp.minimum(pos + 1, w).astype(F32)

    po_ref[...] = _pool_mix(xp, window_sum, cnt, pw_ref, ps_ref)
    xx_ref[0:hist, :] = xx_ref[t:t + hist, :]

    ri = lax.broadcasted_iota(jnp.int32, (t, t), 0)
    ci = lax.broadcasted_iota(jnp.int32, (t, t), 1)
    tri = jnp.where(ci <= ri, 1.0, 0.0).astype(BF16)
    hi, mid, lo = _split3(lf_ref[...])
    fcum = _dot(tri, hi) + _dot(tri, mid) + _dot(tri, lo) + fc_ref[...]
    fc_ref[...] = fcum[t - 1:t, :]
    fh, fm, fl = _split3(fcum)
    ones = jnp.ones((t, LANES), BF16)
    wq = jnp.concatenate([q_ref[...].astype(BF16), fh, fm, fl, ones], axis=1)
    wk = jnp.concatenate([k_ref[...].astype(BF16), fh, fm, fl, ones], axis=1)
    wv = jnp.concatenate([v_ref[...].astype(BF16), ones], axis=1)
    for h in range(N_HEADS):
        qa_ref[h] = _dot(wq, selq_ref[h]).astype(BF16)
        ka_ref[h] = _dot(wk, selk_ref[h]).astype(BF16)
        vt_ref[h] = _dot_nt(selv_ref[h], wv).astype(BF16)


def _poolprep(xp, lf, q, k, v, pw, ps, selq, selk, selv, batch, seq):
    t = SEQ_TILE
    nt = seq // t
    n = batch * seq
    row = lambda b, i: (b * nt + i, 0)
    fix2 = lambda b, i: (0, 0)
    fix3 = lambda b, i: (0, 0, 0)
    return pl.pallas_call(
        _poolprep_body,
        grid=(batch, nt),
        in_specs=[
            pl.BlockSpec((t, D_POOL), row),
            pl.BlockSpec((t, LANES), row),
            pl.BlockSpec((t, D_ATTN), row),
            pl.BlockSpec((t, D_ATTN), row),
            pl.BlockSpec((t, D_ATTN), row),
            pl.BlockSpec(pw.shape, fix3),
            pl.BlockSpec((1, D_POOL), fix2),
            pl.BlockSpec(selq.shape, fix3),
            pl.BlockSpec(selk.shape, fix3),
            pl.BlockSpec(selv.shape, fix3),
        ],
        out_specs=[
            pl.BlockSpec((t, D_POOL), row),
            pl.BlockSpec((N_HEADS, t, AUG), lambda b, i: (0, b * nt + i, 0)),
            pl.BlockSpec((N_HEADS, t, AUG), lambda b, i: (0, b * nt + i, 0)),
            pl.BlockSpec((N_HEADS, AUG, t), lambda b, i: (0, 0, b * nt + i)),
        ],
        out_shape=[
            jax.ShapeDtypeStruct((n, D_POOL), F32),
            jax.ShapeDtypeStruct((N_HEADS, n, AUG), BF16),
            jax.ShapeDtypeStruct((N_HEADS, n, AUG), BF16),
            jax.ShapeDtypeStruct((N_HEADS, AUG, n), BF16),
        ],
        scratch_shapes=[
            pltpu.VMEM((t + 16, D_POOL), F32),
            pltpu.VMEM((1, LANES), F32),
        ],
        compiler_params=_cparams(("arbitrary", "arbitrary")),
        name="poolprep",
    )(xp, lf, q, k, v, pw, ps, selq, selk, selv)


def _flash_body(qa_ref, ka_ref, vt_ref, o_ref, s0_scr, s1_scr):
    tq, tk = Q_TILE, KV_TILE
    i = pl.program_id(2)
    n_full = (i * tq) // tk
    qpos = lax.broadcasted_iota(jnp.int32, (tk, tq), 1) + i * tq
    krel = lax.broadcasted_iota(jnp.int32, (tk, tq), 0)
    vrows = V_ROWS

    s_slots = (s0_scr, s1_scr)

    def scores(j, slot):
        start = pl.multiple_of(j * tk, tk)
        for hh in range(2):
            s_slots[slot][hh] = _dot_nt(ka_ref[hh, pl.ds(start, tk), :], qa_ref[hh])

    def tile(j, slot, carry, masked):
        if not masked:
            scores(j + 1, 1 - slot)
        start = pl.multiple_of(j * tk, tk)
        new = []
        for hh in range(2):
            m, acc = carry[hh]
            s = s_slots[slot][hh]
            if masked:
                s = jnp.where(krel + j * tk <= qpos, s, NEG)
            m_new = jnp.maximum(m, jnp.max(s, axis=0, keepdims=True))
            p = jnp.exp(s - m_new).astype(BF16)
            alpha = jnp.exp(m - m_new)
            acc = alpha * acc + _dot(vt_ref[hh, 0:vrows, pl.ds(start, tk)], p)
            new.append((m_new, acc))
        return tuple(new)

    def pair(jj, carry):
        carry = tile(2 * jj, 0, carry, False)
        return tile(2 * jj + 1, 1, carry, False)

    scores(0, 0)
    one = (jnp.full((1, tq), NEG, F32), jnp.zeros((vrows, tq), F32))
    carry = lax.fori_loop(0, n_full // 2, pair, (one, one))
    j0 = 2 * (n_full // 2)
    carry = lax.cond(
        n_full % 2 == 1,
        lambda c: tile(j0 + 1, 1, tile(j0, 0, c, False), True),
        lambda c: tile(j0, 0, c, True),
        carry)
    outs = []
    for hh in range(2):
        acc = jnp.concatenate([carry[hh][1], jnp.zeros((AUG - vrows, tq), F32)], axis=0)
        acc_t = acc.T
        outs.append(acc_t[:, 0:HEAD_DIM] / acc_t[:, HEAD_DIM:HEAD_DIM + 1])
    o_ref[...] = jnp.concatenate(outs, axis=1)


def _flash(qa, ka, vt, batch, seq):
    tq = Q_TILE
    nq = seq // tq
    n = batch * seq
    return pl.pallas_call(
        _flash_body,
        grid=(batch, N_HEADS // 2, nq),
        in_specs=[
            pl.BlockSpec((2, tq, AUG), lambda b, hp, i: (hp, b * nq + i, 0)),
            pl.BlockSpec((2, seq, AUG), lambda b, hp, i: (hp, b, 0)),
            pl.BlockSpec((2, AUG, seq), lambda b, hp, i: (hp, 0, b)),
        ],
        out_specs=pl.BlockSpec((tq, 2 * HEAD_DIM), lambda b, hp, i: (b * nq + i, hp)),
        out_shape=jax.ShapeDtypeStruct((n, D_ATTN), F32),
        scratch_shapes=[pltpu.VMEM((2, KV_TILE, Q_TILE), F32), pltpu.VMEM((2, KV_TILE, Q_TILE), F32)],
        compiler_params=_cparams(("arbitrary", "arbitrary", "arbitrary")),
        name="flash",
    )(qa, ka, vt)


def _sample_pool_body(xp_ref, st_ref, pw_ref, ps_ref, o_ref, *, start):
    xp = xp_ref[...]

    def window_sum(g, w):
        lanes = slice(g * POOL_GROUP, (g + 1) * POOL_GROUP)
        ws = xp[:, lanes]
        for r in range(1, w):
            ws = ws + st_ref[POOL_STATE - r][:, lanes]
        return ws

    o_ref[...] = _pool_mix(xp, window_sum, lambda w: float(min(start + 1, w)), pw_ref, ps_ref)


def _sample_pool(xp_s, state_t, pw, ps, start):
    return pl.pallas_call(
        functools.partial(_sample_pool_body, start=start),
        out_shape=jax.ShapeDtypeStruct(xp_s.shape, F32),
        compiler_params=pltpu.CompilerParams(vmem_limit_bytes=VMEM_LIMIT),
        name="sample_pool",
    )(xp_s, state_t, pw, ps)


def _sample_attn_body(pt_ref, q_ref, k_ref, v_ref, lf_ref, sfx_ref, *refs, n_steps):
    g_pages = PAGES_PER_STEP
    ck = refs[0:g_pages]
    cv = refs[g_pages:2 * g_pages]
    cl = refs[2 * g_pages:3 * g_pages]
    o_ref = refs[3 * g_pages]
    m_ref, l_ref, acc_ref, car_ref = refs[3 * g_pages + 1:]
    rows = 2 * N_HEADS
    step = pl.program_id(1)

    hrow = lax.broadcasted_iota(jnp.int32, (rows, D_ATTN), 0)
    hcol = lax.broadcasted_iota(jnp.int32, (rows, D_ATTN), 1) // HEAD_DIM
    headmask = hrow == hcol
    qf = q_ref[0].astype(BF16).astype(F32)
    qbd = jnp.where(headmask, qf * ATTN_SCALE, 0.0)

    @pl.when(step == 0)
    def _():
        m_ref[...] = jnp.full(m_ref.shape, NEG, F32)
        l_ref[...] = jnp.zeros_like(l_ref)
        acc_ref[...] = jnp.zeros_like(acc_ref)
        eye = (lax.broadcasted_iota(jnp.int32, (rows, LANES), 0)
               == lax.broadcasted_iota(jnp.int32, (rows, LANES), 1))
        c_new = jnp.sum(jnp.where(eye, lf_ref[0], 0.0), axis=1, keepdims=True)
        car_ref[...] = jnp.broadcast_to(c_new, car_ref.shape)

    carry = car_ref[...]
    sfx = sfx_ref[...]
    bias = []
    for g in range(g_pages):
        lft = cl[g][...]
        hi, mid, lo = _split3(jnp.concatenate([lft, jnp.zeros_like(lft)], axis=0))
        r = _dot(hi, sfx) + _dot(mid, sfx) + _dot(lo, sfx)
        bias.append(r[:, 0:PAGE_SIZE] + carry)
        carry = carry + r[:, PAGE_SIZE:2 * PAGE_SIZE]
    car_ref[...] = carry
    kt_all = jnp.concatenate(
        [ck[g][...].reshape(D_ATTN, PAGE_SIZE).astype(BF16) for g in range(g_pages)], axis=1)
    vt_all = jnp.concatenate(
        [cv[g][...].reshape(D_ATTN, PAGE_SIZE).astype(BF16) for g in range(g_pages)], axis=1)
    s_all = _dot(qbd.astype(BF16), kt_all) + jnp.concatenate(bias, axis=1)
    m_old = m_ref[...]
    m_new = jnp.maximum(m_old, jnp.max(s_all, axis=1, keepdims=True))
    p = jnp.exp(s_all - m_new)
    alpha = jnp.exp(m_old - m_new)
    l_ref[...] = alpha * l_ref[...] + jnp.sum(p, axis=1, keepdims=True)
    acc_ref[...] = alpha * acc_ref[...] + _dot_nt(p.astype(BF16), vt_all)
    m_ref[...] = m_new

    @pl.when(step == n_steps - 1)
    def _():
        kf = k_ref[0].astype(BF16).astype(F32)
        vf = v_ref[0].astype(BF16).astype(F32)
        s_new = jnp.sum(qbd * kf, axis=1, keepdims=True)
        m_old = m_ref[...]
        m_fin = jnp.maximum(m_old, s_new)
        a = jnp.exp(m_old - m_fin)
        p_new = jnp.exp(s_new - m_fin)
        l_fin = a * l_ref[...] + p_new
        acc = a * acc_ref[...] + p_new.astype(BF16).astype(F32) * vf
        o_ref[0] = jnp.sum(jnp.where(headmask, acc / l_fin, 0.0), axis=0, keepdims=True)


def _sample_attn(page_table, q_s, k_s, v_s, lf_s, sfx, cache_kt, cache_vt, cache_lft, layer):
    n_seq, n_pages = page_table.shape
    g_pages = PAGES_PER_STEP
    n_steps = n_pages // g_pages
    rows = 2 * N_HEADS

    def page_map(g, nd):
        def index(b, s, pt):
            return (layer, pt[b, n_pages - 1 - (s * g_pages + g)]) + (0,) * nd
        return index

    tok = lambda b, s, pt: (b, 0, 0)
    in_specs = [
        pl.BlockSpec((1, 1, D_ATTN), tok),
        pl.BlockSpec((1, 1, D_ATTN), tok),
        pl.BlockSpec((1, 1, D_ATTN), tok),
        pl.BlockSpec((1, 1, LANES), tok),
        pl.BlockSpec(sfx.shape, lambda b, s, pt: (0, 0)),
    ]
    page = (None, None, N_HEADS, HEAD_DIM, PAGE_SIZE)
    in_specs += [pl.BlockSpec(page, page_map(g, 3)) for g in range(g_pages)]
    in_specs += [pl.BlockSpec(page, page_map(g, 3)) for g in range(g_pages)]
    in_specs += [pl.BlockSpec((None, None, N_HEADS, PAGE_SIZE), page_map(g, 2)) for g in range(g_pages)]
    grid_spec = pltpu.PrefetchScalarGridSpec(
        num_scalar_prefetch=1,
        grid=(n_seq, n_steps),
        in_specs=in_specs,
        out_specs=pl.BlockSpec((1, 1, D_ATTN), tok),
        scratch_shapes=[
            pltpu.VMEM((rows, 1), F32),
            pltpu.VMEM((rows, 1), F32),
            pltpu.VMEM((rows, D_ATTN), F32),
            pltpu.VMEM((rows, LANES), F32),
        ],
    )
    return pl.pallas_call(
        functools.partial(_sample_attn_body, n_steps=n_steps),
        grid_spec=grid_spec,
        out_shape=jax.ShapeDtypeStruct((n_seq, 1, D_ATTN), F32),
        compiler_params=_cparams(("arbitrary", "arbitrary")),
        name="sample_attn",
    )(page_table, q_s, k_s, v_s, lf_s, sfx,
      *([cache_kt] * g_pages), *([cache_vt] * g_pages), *([cache_lft] * g_pages))


def _outproj_body(x_ref, po_ref, at_ref, wo_ref, g_ref, x1_ref, xn_ref):
    mix = jnp.concatenate([po_ref[...], at_ref[...]], axis=1).astype(BF16)
    x1 = x_ref[...] + _dot(mix, wo_ref[...])
    x1_ref[...] = x1
    ms = jnp.mean(x1 * x1, axis=-1, keepdims=True)
    xn_ref[...] = (x1 * lax.rsqrt(ms + EPS) * g_ref[...]).astype(BF16)


def _outproj(x, po, at, wo, g):
    n = x.shape[0]
    t = TOK_TILE
    row = lambda i: (i, 0)
    fix = lambda i: (0, 0)
    return pl.pallas_call(
        _outproj_body,
        grid=(n // t,),
        in_specs=[
            pl.BlockSpec((t, D_MODEL), row),
            pl.BlockSpec((t, D_POOL), row),
            pl.BlockSpec((t, D_ATTN), row),
            pl.BlockSpec(wo.shape, fix),
            pl.BlockSpec((1, D_MODEL), fix),
        ],
        out_specs=[pl.BlockSpec((t, D_MODEL), row), pl.BlockSpec((t, D_MODEL), row)],
        out_shape=[jax.ShapeDtypeStruct((n, D_MODEL), F32), jax.ShapeDtypeStruct((n, D_MODEL), BF16)],
        compiler_params=_cparams(("arbitrary",)),
        name="outproj",
    )(x, po, at, wo, g)


def _top16_ranks(s, sv_ref, half, rowi):
    rank = jnp.full(s.shape, float(PEER_TOPK), F32)
    for c in range(PEER_TOPK):
        m = jnp.max(s, axis=0, keepdims=True)
        first = jnp.min(jnp.where(s == m, rowi, float(N_KEYS)), axis=0, keepdims=True)
        sel = rowi == first
        rank = jnp.where(sel, float(c), rank)
        s = jnp.where(sel, -jnp.inf, s)
        sv_ref[half, c:c + 1, :] = m
    return rank


def _top16_ranks_distinct(s, sv_ref, half):
    rank = jnp.full(s.shape, float(PEER_TOPK), F32)
    for c in range(PEER_TOPK):
        m = jnp.max(s, axis=0, keepdims=True)
        sel = s == m
        rank = jnp.where(sel, float(c), rank)
        s = jnp.where(sel, -jnp.inf, s)
        sv_ref[half, c:c + 1, :] = m
    return rank


def _pair_segments():
    segs, row = [], 0
    for c in range(PEER_TOPK):
        nd = PEER_TOPK // (c + 1)
        if nd < 8:
            break
        segs.append((row, c, nd))
        row += nd
    bins = []
    for c in range(len(segs), PEER_TOPK):
        nd = PEER_TOPK // (c + 1)
        for bn in bins:
            if bn[0] + nd <= 8:
                bn[1].append((bn[0], c, nd))
                bn[0] += nd
                break
        else:
            bins.append([nd, [(0, c, nd)]])
    for bn in bins:
        segs += [(row + o, c, nd) for o, c, nd in bn[1]]
        row += 8
    return segs, row


def _peer_route_body(xn_ref, wq_ref, sk_ref, flat_ref, a_ref, n_ref, b_ref, r_ref,
                     q_scr, sv_ref, rk_ref, cand_scr, wgt_scr):
    t = xn_ref.shape[0]
    q = _dot(xn_ref[...], wq_ref[...]).astype(BF16)
    for hp in range(2 * PEER_HEADS):
        q_scr[hp] = q[:, hp * D_HALF:(hp + 1) * D_HALF]
    rowi = lax.broadcasted_iota(jnp.int32, (N_KEYS, t), 0).astype(F32)
    segs, n_rows = _pair_segments()
    flat = flat_ref[...]
    big = float(PEER_TOPK * PEER_TOPK)
    cand_scr[...] = jnp.full(cand_scr.shape, -jnp.inf, F32)
    wgt_scr[...] = jnp.zeros_like(wgt_scr)

    def head(h, _):
        s1 = _dot_nt(sk_ref[0], q_scr[2 * h])
        s2 = _dot_nt(sk_ref[1], q_scr[2 * h + 1])
        rk_ref[0] = _top16_ranks_distinct(s1, sv_ref, 0)
        rk_ref[1] = _top16_ranks_distinct(s2, sv_ref, 1)
        ranked = (jnp.sum(jnp.where(rk_ref[0] < float(PEER_TOPK), 1.0, 0.0), axis=0, keepdims=True)
                  + jnp.sum(jnp.where(rk_ref[1] < float(PEER_TOPK), 1.0, 0.0), axis=0, keepdims=True))

        @pl.when(jnp.max(jnp.abs(ranked - float(2 * PEER_TOPK))) > 0.0)
        def _():
            rk_ref[0] = _top16_ranks(s1, sv_ref, 0, rowi)
            rk_ref[1] = _top16_ranks(s2, sv_ref, 1, rowi)

        r1 = rk_ref[0]
        r2 = rk_ref[1]
        sv1 = sv_ref[0]
        sv2 = sv_ref[1]
        e1 = jnp.exp(sv1 - sv1[0:1, :])
        e2 = jnp.exp(sv2 - sv2[0:1, :])
        for row, c, nd in segs:
            cand_scr[row:row + nd, :] = sv1[c:c + 1, :] + sv2[0:nd, :]
            wgt_scr[row:row + nd, :] = e1[c:c + 1, :] * e2[0:nd, :]
        cand = cand_scr[...]
        chosen = jnp.zeros(cand.shape, F32)
        for _k in range(PEER_TOPK):
            m = jnp.max(cand, axis=0, keepdims=True)
            first = jnp.min(jnp.where(cand == m, flat, big), axis=0, keepdims=True)
            sel = flat == first
            chosen = jnp.where(sel, 1.0, chosen)
            cand = jnp.where(sel, -jnp.inf, cand)
        z = jnp.sum(chosen * wgt_scr[...], axis=0, keepdims=True)
        nrow = jnp.zeros((N_KEYS, t), F32)
        for row, c, nd in segs:
            n_c = jnp.sum(chosen[row:row + nd, :], axis=0, keepdims=True)
            nrow = jnp.where(r1 == float(c), n_c, nrow)
        a_ref[h] = jnp.where(r1 < float(PEER_TOPK), jnp.exp(s1 - sv1[0:1, :]) / z, 0.0)
        n_ref[h] = nrow
        b_ref[h] = jnp.exp(s2 - sv2[0:1, :]).astype(BF16)
        r_ref[h] = r2.astype(BF16)
        return 0

    lax.fori_loop(0, PEER_HEADS, head, 0)


def _peer_route(xn, wq, sk):
    n = xn.shape[0]
    t = SEQ_TILE
    fac = pl.BlockSpec((PEER_HEADS, N_KEYS, t), lambda i: (0, 0, i))
    shp = jax.ShapeDtypeStruct((PEER_HEADS, N_KEYS, n), F32)
    shp_bf = jax.ShapeDtypeStruct((PEER_HEADS, N_KEYS, n), BF16)
    segs, n_rows = _pair_segments()
    flat = np.full((n_rows, t), 1e9, np.float32)
    for row, c, nd in segs:
        flat[row:row + nd, :] = (c * PEER_TOPK + np.arange(nd, dtype=np.float32))[:, None]
    return pl.pallas_call(
        _peer_route_body,
        grid=(n // t,),
        in_specs=[
            pl.BlockSpec((t, D_MODEL), lambda i: (i, 0)),
            pl.BlockSpec(wq.shape, lambda i: (0, 0)),
            pl.BlockSpec(sk.shape, lambda i: (0, 0, 0)),
            pl.BlockSpec(flat.shape, lambda i: (0, 0)),
        ],
        out_specs=[fac, fac, fac, fac],
        out_shape=[shp, shp, shp_bf, shp_bf],
        scratch_shapes=[
            pltpu.VMEM((2 * PEER_HEADS, t, D_HALF), BF16),
            pltpu.VMEM((2, PEER_TOPK, t), F32),
            pltpu.VMEM((2, N_KEYS, t), F32),
            pltpu.VMEM((n_rows, t), F32),
            pltpu.VMEM((n_rows, t), F32),
        ],
        compiler_params=_cparams(("arbitrary",)),
        name="peer_route",
    )(xn, wq, sk, jnp.asarray(flat))


def _peer_expert_body(xn_ref, a_ref, n_ref, b_ref, r_ref, u_ref, vt_ref, x1_ref,
                      o_ref, acc_ref, *, n_blocks):
    e = pl.program_id(1)

    @pl.when(e == 0)
    def _():
        acc_ref[...] = jnp.zeros_like(acc_ref)

    a_rows = a_ref[...].astype(BF16)
    n_rows = n_ref[...].astype(BF16)
    zero = jnp.zeros((), BF16)
    xn = xn_ref[...]
    se = SUB_ROWS * N_KEYS
    acts = [_dot_nt(u_ref[s * se:(s + 1) * se, :], xn) for s in range(EXP_ROWS // SUB_ROWS)]
    total = None
    for s, act in enumerate(acts):
        parts = []
        for k in range(SUB_ROWS):
            i = s * SUB_ROWS + k
            ge = jax.nn.gelu(act[k * N_KEYS:(k + 1) * N_KEYS, :], approximate=True).astype(BF16)
            gate = None
            for h in range(PEER_HEADS):
                keep = r_ref[h] < n_rows[h, i:i + 1, :]
                term = jnp.where(keep, b_ref[h], zero) * a_rows[h, i:i + 1, :]
                gate = term if gate is None else gate + term
            parts.append(gate * ge)
        w = jnp.concatenate(parts, axis=0)
        d = _dot(vt_ref[:, s * se:(s + 1) * se], w)
        total = d if total is None else total + d
    acc_ref[...] += total

    @pl.when(e == n_blocks - 1)
    def _():
        o_ref[...] = x1_ref[...] + acc_ref[...].T


def _peer_expert(xn, fa, fn, fb, fr, u, vt, x1):
    n = xn.shape[0]
    t = TOK_TILE
    eb = EXP_ROWS * N_KEYS
    n_blocks = u.shape[0] // eb
    tok = lambda i, e: (i, 0)
    return pl.pallas_call(
        functools.partial(_peer_expert_body, n_blocks=n_blocks),
        grid=(n // t, n_blocks),
        in_specs=[
            pl.BlockSpec((t, D_MODEL), tok),
            pl.BlockSpec((PEER_HEADS, EXP_ROWS, t), lambda i, e: (0, e, i)),
            pl.BlockSpec((PEER_HEADS, EXP_ROWS, t), lambda i, e: (0, e, i)),
            pl.BlockSpec((PEER_HEADS, N_KEYS, t), lambda i, e: (0, 0, i)),
            pl.BlockSpec((PEER_HEADS, N_KEYS, t), lambda i, e: (0, 0, i)),
            pl.BlockSpec((eb, D_MODEL), lambda i, e: (e, 0)),
            pl.BlockSpec((D_MODEL, eb), lambda i, e: (0, e)),
            pl.BlockSpec((t, D_MODEL), tok),
        ],
        out_specs=pl.BlockSpec((t, D_MODEL), tok),
        out_shape=jax.ShapeDtypeStruct((n, D_MODEL), F32),
        scratch_shapes=[pltpu.VMEM((D_MODEL, t), F32)],
        compiler_params=_cparams(("arbitrary", "arbitrary")),
        name="peer_expert",
    )(xn, fa, fn, fb, fr, u, vt, x1)


def _selectors():
    wide = D_ATTN + 4 * LANES
    selq = np.zeros((N_HEADS, wide, AUG), np.float32)
    selk = np.zeros((N_HEADS, wide, AUG), np.float32)
    selv = np.zeros((N_HEADS, AUG, D_ATTN + LANES), np.float32)
    ones_row = D_ATTN + 3 * LANES
    for h in range(N_HEADS):
        for d in range(HEAD_DIM):
            selq[h, h * HEAD_DIM + d, d] = ATTN_SCALE
            selk[h, h * HEAD_DIM + d, d] = 1.0
            selv[h, d, h * HEAD_DIM + d] = 1.0
        for piece in range(3):
            selq[h, D_ATTN + piece * LANES + h, HEAD_DIM + piece] = 1.0
            selq[h, ones_row, HEAD_DIM + 3 + piece] = 1.0
            selk[h, ones_row, HEAD_DIM + piece] = 1.0
            selk[h, D_ATTN + piece * LANES + h, HEAD_DIM + 3 + piece] = -1.0
        selv[h, HEAD_DIM, D_ATTN] = 1.0
    return (jnp.asarray(selq, BF16), jnp.asarray(selk, BF16), jnp.asarray(selv, BF16))


def kernel(x_prompt, x_sample, cache_k, cache_v, cache_logf, state_pool, page_table,
           norm_mix, w_in, b_forget, q_norm, k_norm, pool_w, pool_scale, w_out,
           norm_ffn, peer_wq, peer_subkeys, peer_u, peer_v):
    batch, seq, _ = x_prompt.shape
    n_seq, dec_seq, _ = x_sample.shape
    depth = w_in.shape[0]
    n_pool = cache_k.shape[1]
    n_pages = page_table.shape[1]
    past = n_pages * PAGE_SIZE
    assert dec_seq == 1 and seq % KV_TILE == 0 and n_pages % PAGES_PER_STEP == 0
    assert n_seq % 8 == 0
    n_prompt = batch * seq
    n_tok = n_prompt + n_seq
    n_all = -(-n_tok // TOK_TILE) * TOK_TILE

    x = jnp.concatenate([x_prompt.reshape(n_prompt, D_MODEL), x_sample.reshape(n_seq, D_MODEL),
                         jnp.zeros((n_all - n_tok, D_MODEL), F32)], axis=0)
    selq, selk, selv = _selectors()
    hm = jnp.asarray(np.kron(np.eye(N_HEADS), np.full((HEAD_DIM, HEAD_DIM), 1.0 / HEAD_DIM)), BF16)
    jj = np.arange(PAGE_SIZE)
    sfx = jnp.asarray(np.concatenate([(jj[:, None] > jj[None, :]).astype(np.float32),
                                      np.ones((PAGE_SIZE, PAGE_SIZE), np.float32)], axis=1), BF16)
    cache_kt = jnp.transpose(cache_k, (0, 1, 3, 4, 2))
    cache_vt = jnp.transpose(cache_v, (0, 1, 3, 4, 2))
    cache_lft = jnp.transpose(cache_logf, (0, 1, 3, 2))

    kp_l, vp_l, fp_l, pp_l, ks_l, vs_l, fs_l, ps_l = [], [], [], [], [], [], [], []
    for l in range(depth):
        wm = w_in[l][:, :D_POOL + 3 * D_ATTN].astype(BF16)
        wf = jnp.pad(w_in[l][:, D_POOL + 3 * D_ATTN:], ((0, 0), (0, LANES - N_HEADS))).astype(BF16)
        bfp = jnp.pad(b_forget[l], (0, LANES - N_HEADS)).reshape(1, LANES)
        xp, q, k, v, lf = _inproj(x, norm_mix[l].reshape(1, D_MODEL), wm, wf, bfp,
                                  jnp.tile(q_norm[l], N_HEADS).reshape(1, D_ATTN),
                                  jnp.tile(k_norm[l], N_HEADS).reshape(1, D_ATTN), hm)
        pw = pool_w[l].astype(BF16)
        ps = pool_scale[l].reshape(1, D_POOL)

        po_p, qa, ka, vt = _poolprep(xp, lf, q, k, v, pw, ps, selq, selk, selv, batch, seq)
        at_p = _flash(qa, ka, vt, batch, seq)

        sl = slice(n_prompt, n_tok)
        xp_s = xp[sl]
        po_s = _sample_pool(xp_s, jnp.transpose(state_pool[l], (1, 0, 2)), pw, ps, past)
        row = lambda t: t[sl].reshape(n_seq, 1, t.shape[1])
        at_s = _sample_attn(page_table, row(q), row(k), row(v), row(lf), sfx,
                            cache_kt, cache_vt, cache_lft, l).reshape(n_seq, D_ATTN)

        pad = jnp.zeros((n_all - n_tok, D_POOL), F32)
        po = jnp.concatenate([po_p, po_s, pad], axis=0)
        at = jnp.concatenate([at_p, at_s, pad], axis=0)
        x1, xn = _outproj(x, po, at, w_out[l].astype(BF16), norm_ffn[l].reshape(1, D_MODEL))
        fa, fn, fb, fr = _peer_route(xn, peer_wq[l].astype(BF16), peer_subkeys[l].astype(BF16))
        x = _peer_expert(xn, fa, fn, fb, fr, peer_u[l].astype(BF16),
                         jnp.transpose(peer_v[l]).astype(BF16), x1)

        kp_l.append(k[:n_prompt].reshape(batch, seq, N_HEADS, HEAD_DIM))
        vp_l.append(v[:n_prompt].reshape(batch, seq, N_HEADS, HEAD_DIM))
        fp_l.append(lf[:n_prompt, :N_HEADS].reshape(batch, seq, N_HEADS))
        pp_l.append(xp[:n_prompt].reshape(batch, seq, D_POOL)[:, seq - POOL_STATE:])
        ks_l.append(k[sl].reshape(n_seq, 1, N_HEADS, HEAD_DIM))
        vs_l.append(v[sl].reshape(n_seq, 1, N_HEADS, HEAD_DIM))
        fs_l.append(lf[sl, :N_HEADS].reshape(n_seq, 1, N_HEADS))
        ps_l.append(jnp.concatenate([state_pool[l][:, 1:], xp_s[:, None, :]], axis=1))

    return (x[:n_prompt].reshape(batch, seq, D_MODEL), x[sl].reshape(n_seq, 1, D_MODEL),
            jnp.stack(kp_l), jnp.stack(vp_l), jnp.stack(fp_l), jnp.stack(pp_l),
            jnp.stack(ks_l), jnp.stack(vs_l), jnp.stack(fs_l), jnp.stack(ps_l))
```

```python
import functools

import jax
import jax.numpy as jnp
import numpy as np
from jax import lax
from jax.experimental import pallas as pl
from jax.experimental.pallas import tpu as pltpu

F32 = jnp.float32
BF16 = jnp.bfloat16

EPS = 1e-6
D_MODEL = 1024
D_POOL = 512
POOL_WINDOWS = (2, 4, 8, 16)
POOL_GROUP = 128
POOL_STATE = 15
N_HEADS = 8
HEAD_DIM = 64
D_ATTN = N_HEADS * HEAD_DIM
PAGE_SIZE = 128
PEER_HEADS = 8
PEER_TOPK = 16
N_KEYS = 128
D_HALF = 128
ATTN_SCALE = HEAD_DIM ** -0.5

LANES = 128
AUG = 128
V_ROWS = 80
TOK_TILE = 512
SEQ_TILE = 256
Q_TILE = 256
KV_TILE = 512
EXP_ROWS = 8
PAGES_PER_STEP = 16
VMEM_LIMIT = 56 * 1024 * 1024
NEG = -1e30


def _cparams(sem):
    return pltpu.CompilerParams(dimension_semantics=sem, vmem_limit_bytes=VMEM_LIMIT)


def _split3(x):
    hi = x.astype(BF16)
    r = x - hi.astype(F32)
    mid = r.astype(BF16)
    lo = (r - mid.astype(F32)).astype(BF16)
    return hi, mid, lo


def _dot(a, b):
    return jnp.dot(a, b, preferred_element_type=F32)


def _dot_nt(a, b):
    return lax.dot_general(a, b, (((1,), (1,)), ((), ())), preferred_element_type=F32)


def _inproj_body(x_ref, g_ref, wm_ref, wf_ref, bf_ref, qg_ref, kg_ref, hm_ref,
                 xp_ref, q_ref, k_ref, v_ref, lf_ref):
    x = x_ref[...]
    ms = jnp.mean(x * x, axis=-1, keepdims=True)
    h = (x * lax.rsqrt(ms + EPS) * g_ref[...]).astype(BF16)
    z = _dot(h, wm_ref[...])
    xp_ref[...] = z[:, 0:D_POOL]
    hm = hm_ref[...]

    def headnorm(t, gain):
        sq = t * t
        hi = sq.astype(BF16)
        lo = (sq - hi.astype(F32)).astype(BF16)
        msh = _dot(hi, hm) + _dot(lo, hm)
        return t * lax.rsqrt(msh + EPS) * gain

    q_ref[...] = headnorm(z[:, D_POOL:D_POOL + D_ATTN], qg_ref[...])
    k_ref[...] = headnorm(z[:, D_POOL + D_ATTN:D_POOL + 2 * D_ATTN], kg_ref[...])
    v_ref[...] = z[:, D_POOL + 2 * D_ATTN:D_POOL + 3 * D_ATTN]
    f = _dot(h, wf_ref[...]) + bf_ref[...]
    lf_ref[...] = jnp.minimum(f, 0.0) - jnp.log1p(jnp.exp(-jnp.abs(f)))


def _inproj(x, g, wm, wf, bfp, qg, kg, hm):
    n = x.shape[0]
    t = TOK_TILE
    row = lambda i: (i, 0)
    fix = lambda i: (0, 0)
    return pl.pallas_call(
        _inproj_body,
        grid=(n // t,),
        in_specs=[
            pl.BlockSpec((t, D_MODEL), row),
            pl.BlockSpec((1, D_MODEL), fix),
            pl.BlockSpec(wm.shape, fix),
            pl.BlockSpec(wf.shape, fix),
            pl.BlockSpec((1, LANES), fix),
            pl.BlockSpec((1, D_ATTN), fix),
            pl.BlockSpec((1, D_ATTN), fix),
            pl.BlockSpec(hm.shape, fix),
        ],
        out_specs=[
            pl.BlockSpec((t, D_POOL), row),
            pl.BlockSpec((t, D_ATTN), row),
            pl.BlockSpec((t, D_ATTN), row),
            pl.BlockSpec((t, D_ATTN), row),
            pl.BlockSpec((t, LANES), row),
        ],
        out_shape=[
            jax.ShapeDtypeStruct((n, D_POOL), F32),
            jax.ShapeDtypeStruct((n, D_ATTN), F32),
            jax.ShapeDtypeStruct((n, D_ATTN), F32),
            jax.ShapeDtypeStruct((n, D_ATTN), F32),
            jax.ShapeDtypeStruct((n, LANES), F32),
        ],
        compiler_params=_cparams(("arbitrary",)),
        name="inproj",
    )(x, g, wm, wf, bfp, qg, kg, hm)


def _pool_mix(xp, window_sum, cnt_fn, pw_ref, ps_ref):
    outs = []
    for g, w in enumerate(POOL_WINDOWS):
        lanes = slice(g * POOL_GROUP, (g + 1) * POOL_GROUP)
        pooled = window_sum(g, w) / cnt_fn(w) - xp[:, lanes]
        outs.append(_dot(pooled.astype(BF16), pw_ref[g]))
    return jnp.concatenate(outs, axis=1) * ps_ref[...]


def _poolprep_body(xp_ref, lf_ref, q_ref, k_ref, v_ref, pw_ref, ps_ref,
                   selq_ref, selk_ref, selv_ref,
                   po_ref, qa_ref, ka_ref, vt_ref, xx_ref, fc_ref):
    t = SEQ_TILE
    hist = 16
    step = pl.program_id(1)

    @pl.when(step == 0)
    def _():
        xx_ref[0:hist, :] = jnp.zeros((hist, D_POOL), F32)
        fc_ref[...] = jnp.zeros_like(fc_ref)

    xp = xp_ref[...]
    xx_ref[hist:hist + t, :] = xp
    pos = lax.broadcasted_iota(jnp.int32, (t, POOL_GROUP), 0) + step * t

    def window_sum(g, w):
        lanes = slice(g * POOL_GROUP, (g + 1) * POOL_GROUP)
        ws = xp[:, lanes]
        for r in range(1, w):
            ws = ws + xx_ref[hist - r:hist - r + t, lanes]
        return ws

    def cnt(w):
        return jnp.minimum(pos + 1, w).astype(F32)

    po_ref[...] = _pool_mix(xp, window_sum, cnt, pw_ref, ps_ref)
    xx_ref[0:hist, :] = xx_ref[t:t + hist, :]

    ri = lax.broadcasted_iota(jnp.int32, (t, t), 0)
    ci = lax.broadcasted_iota(jnp.int32, (t, t), 1)
    tri = jnp.where(ci <= ri, 1.0, 0.0).astype(BF16)
    hi, mid, lo = _split3(lf_ref[...])
    fcum = _dot(tri, hi) + _dot(tri, mid) + _dot(tri, lo) + fc_ref[...]
    fc_ref[...] = fcum[t - 1:t, :]
    fh, fm, fl = _split3(fcum)
    ones = jnp.ones((t, LANES), BF16)
    wq = jnp.concatenate([q_ref[...].astype(BF16), fh, fm, fl, ones], axis=1)
    wk = jnp.concatenate([k_ref[...].astype(BF16), fh, fm, fl, ones], axis=1)
    wv = jnp.concatenate([v_ref[...].astype(BF16), ones], axis=1)
    for h in range(N_HEADS):
        qa_ref[h] = _dot(wq, selq_ref[h]).astype(BF16)
        ka_ref[h] = _dot(wk, selk_ref[h]).astype(BF16)
        vt_ref[h] = _dot_nt(selv_ref[h], wv).astype(BF16)


def _poolprep(xp, lf, q, k, v, pw, ps, selq, selk, selv, batch, seq):
    t = SEQ_TILE
    nt = seq // t
    n = batch * seq
    row = lambda b, i: (b * nt + i, 0)
    fix2 = lambda b, i: (0, 0)
    fix3 = lambda b, i: (0, 0, 0)
    return pl.pallas_call(
        _poolprep_body,
        grid=(batch, nt),
        in_specs=[
            pl.BlockSpec((t, D_POOL), row),
            pl.BlockSpec((t, LANES), row),
            pl.BlockSpec((t, D_ATTN), row),
            pl.BlockSpec((t, D_ATTN), row),
            pl.BlockSpec((t, D_ATTN), row),
            pl.BlockSpec(pw.shape, fix3),
            pl.BlockSpec((1, D_POOL), fix2),
            pl.BlockSpec(selq.shape, fix3),
            pl.BlockSpec(selk.shape, fix3),
            pl.BlockSpec(selv.shape, fix3),
        ],
        out_specs=[
            pl.BlockSpec((t, D_POOL), row),
            pl.BlockSpec((N_HEADS, t, AUG), lambda b, i: (0, b * nt + i, 0)),
            pl.BlockSpec((N_HEADS, t, AUG), lambda b, i: (0, b * nt + i, 0)),
            pl.BlockSpec((N_HEADS, AUG, t), lambda b, i: (0, 0, b * nt + i)),
        ],
        out_shape=[
            jax.ShapeDtypeStruct((n, D_POOL), F32),
            jax.ShapeDtypeStruct((N_HEADS, n, AUG), BF16),
            jax.ShapeDtypeStruct((N_HEADS, n, AUG), BF16),
            jax.ShapeDtypeStruct((N_HEADS, AUG, n), BF16),
        ],
        scratch_shapes=[
            pltpu.VMEM((t + 16, D_POOL), F32),
            pltpu.VMEM((1, LANES), F32),
        ],
        compiler_params=_cparams(("arbitrary", "arbitrary")),
        name="poolprep",
    )(xp, lf, q, k, v, pw, ps, selq, selk, selv)


def _flash_body(qa_ref, ka_ref, vt_ref, o_ref, s0_scr, s1_scr):
    tq, tk = Q_TILE, KV_TILE
    i = pl.program_id(2)
    n_full = (i * tq) // tk
    qpos = lax.broadcasted_iota(jnp.int32, (tk, tq), 1) + i * tq
    krel = lax.broadcasted_iota(jnp.int32, (tk, tq), 0)
    vrows = V_ROWS

    s_slots = (s0_scr, s1_scr)

    def scores(j, slot):
        start = pl.multiple_of(j * tk, tk)
        for hh in range(2):
            s_slots[slot][hh] = _dot_nt(ka_ref[hh, pl.ds(start, tk), :], qa_ref[hh])

    def tile(j, slot, carry, masked):
        if not masked:
            scores(j + 1, 1 - slot)
        start = pl.multiple_of(j * tk, tk)
        new = []
        for hh in range(2):
            m, acc = carry[hh]
            s = s_slots[slot][hh]
            if masked:
                s = jnp.where(krel + j * tk <= qpos, s, NEG)
            m_new = jnp.maximum(m, jnp.max(s, axis=0, keepdims=True))
            p = jnp.exp(s - m_new).astype(BF16)
            alpha = jnp.exp(m - m_new)
            acc = alpha * acc + _dot(vt_ref[hh, 0:vrows, pl.ds(start, tk)], p)
            new.append((m_new, acc))
        return tuple(new)

    def pair(jj, carry):
        carry = tile(2 * jj, 0, carry, False)
        return tile(2 * jj + 1, 1, carry, False)

    scores(0, 0)
    one = (jnp.full((1, tq), NEG, F32), jnp.zeros((vrows, tq), F32))
    carry = lax.fori_loop(0, n_full // 2, pair, (one, one))
    j0 = 2 * (n_full // 2)
    carry = lax.cond(
        n_full % 2 == 1,
        lambda c: tile(j0 + 1, 1, tile(j0, 0, c, False), True),
        lambda c: tile(j0, 0, c, True),
        carry)
    outs = []
    for hh in range(2):
        acc = jnp.concatenate([carry[hh][1], jnp.zeros((AUG - vrows, tq), F32)], axis=0)
        acc_t = acc.T
        outs.append(acc_t[:, 0:HEAD_DIM] / acc_t[:, HEAD_DIM:HEAD_DIM + 1])
    o_ref[...] = jnp.concatenate(outs, axis=1)


def _flash(qa, ka, vt, batch, seq):
    tq = Q_TILE
    nq = seq // tq
    n = batch * seq
    return pl.pallas_call(
        _flash_body,
        grid=(batch, N_HEADS // 2, nq),
        in_specs=[
            pl.BlockSpec((2, tq, AUG), lambda b, hp, i: (hp, b * nq + i, 0)),
            pl.BlockSpec((2, seq, AUG), lambda b, hp, i: (hp, b, 0)),
            pl.BlockSpec((2, AUG, seq), lambda b, hp, i: (hp, 0, b)),
        ],
        out_specs=pl.BlockSpec((tq, 2 * HEAD_DIM), lambda b, hp, i: (b * nq + i, hp)),
        out_shape=jax.ShapeDtypeStruct((n, D_ATTN), F32),
        scratch_shapes=[pltpu.VMEM((2, KV_TILE, Q_TILE), F32), pltpu.VMEM((2, KV_TILE, Q_TILE), F32)],
        compiler_params=_cparams(("arbitrary", "arbitrary", "arbitrary")),
        name="flash",
    )(qa, ka, vt)


def _sample_pool_body(xp_ref, st_ref, pw_ref, ps_ref, o_ref, *, start):
    xp = xp_ref[...]

    def window_sum(g, w):
        lanes = slice(g * POOL_GROUP, (g + 1) * POOL_GROUP)
        ws = xp[:, lanes]
        for r in range(1, w):
            ws = ws + st_ref[POOL_STATE - r][:, lanes]
        return ws

    o_ref[...] = _pool_mix(xp, window_sum, lambda w: float(min(start + 1, w)), pw_ref, ps_ref)


def _sample_pool(xp_s, state_t, pw, ps, start):
    return pl.pallas_call(
        functools.partial(_sample_pool_body, start=start),
        out_shape=jax.ShapeDtypeStruct(xp_s.shape, F32),
        compiler_params=pltpu.CompilerParams(vmem_limit_bytes=VMEM_LIMIT),
        name="sample_pool",
    )(xp_s, state_t, pw, ps)


def _sample_attn_body(pt_ref, q_ref, k_ref, v_ref, lf_ref, sfx_ref, *refs, n_steps):
    g_pages = PAGES_PER_STEP
    ck = refs[0:g_pages]
    cv = refs[g_pages:2 * g_pages]
    cl = refs[2 * g_pages:3 * g_pages]
    o_ref = refs[3 * g_pages]
    m_ref, l_ref, acc_ref, car_ref = refs[3 * g_pages + 1:]
    rows = 2 * N_HEADS
    step = pl.program_id(1)

    hrow = lax.broadcasted_iota(jnp.int32, (rows, D_ATTN), 0)
    hcol = lax.broadcasted_iota(jnp.int32, (rows, D_ATTN), 1) // HEAD_DIM
    headmask = hrow == hcol
    qf = q_ref[0].astype(BF16).astype(F32)
    qbd = jnp.where(headmask, qf * ATTN_SCALE, 0.0)

    @pl.when(step == 0)
    def _():
        m_ref[...] = jnp.full(m_ref.shape, NEG, F32)
        l_ref[...] = jnp.zeros_like(l_ref)
        acc_ref[...] = jnp.zeros_like(acc_ref)
        eye = (lax.broadcasted_iota(jnp.int32, (rows, LANES), 0)
               == lax.broadcasted_iota(jnp.int32, (rows, LANES), 1))
        c_new = jnp.sum(jnp.where(eye, lf_ref[0], 0.0), axis=1, keepdims=True)
        car_ref[...] = jnp.broadcast_to(c_new, car_ref.shape)

    carry = car_ref[...]
    sfx = sfx_ref[...]
    bias = []
    for g in range(g_pages):
        lft = cl[g][...]
        hi, mid, lo = _split3(jnp.concatenate([lft, jnp.zeros_like(lft)], axis=0))
        r = _dot(hi, sfx) + _dot(mid, sfx) + _dot(lo, sfx)
        bias.append(r[:, 0:PAGE_SIZE] + carry)
        carry = carry + r[:, PAGE_SIZE:2 * PAGE_SIZE]
    car_ref[...] = carry
    kt_all = jnp.concatenate(
        [ck[g][...].reshape(D_ATTN, PAGE_SIZE).astype(BF16) for g in range(g_pages)], axis=1)
    vt_all = jnp.concatenate(
        [cv[g][...].reshape(D_ATTN, PAGE_SIZE).astype(BF16) for g in range(g_pages)], axis=1)
    s_all = _dot(qbd.astype(BF16), kt_all) + jnp.concatenate(bias, axis=1)
    m_old = m_ref[...]
    m_new = jnp.maximum(m_old, jnp.max(s_all, axis=1, keepdims=True))
    p = jnp.exp(s_all - m_new)
    alpha = jnp.exp(m_old - m_new)
    l_ref[...] = alpha * l_ref[...] + jnp.sum(p, axis=1, keepdims=True)
    acc_ref[...] = alpha * acc_ref[...] + _dot_nt(p.astype(BF16), vt_all)
    m_ref[...] = m_new

    @pl.when(step == n_steps - 1)
    def _():
        kf = k_ref[0].astype(BF16).astype(F32)
        vf = v_ref[0].astype(BF16).astype(F32)
        s_new = jnp.sum(qbd * kf, axis=1, keepdims=True)
        m_old = m_ref[...]
        m_fin = jnp.maximum(m_old, s_new)
        a = jnp.exp(m_old - m_fin)
        p_new = jnp.exp(s_new - m_fin)
        l_fin = a * l_ref[...] + p_new
        acc = a * acc_ref[...] + p_new.astype(BF16).astype(F32) * vf
        o_ref[0] = jnp.sum(jnp.where(headmask, acc / l_fin, 0.0), axis=0, keepdims=True)


def _sample_attn(page_table, q_s, k_s, v_s, lf_s, sfx, cache_kt, cache_vt, cache_lft, layer):
    n_seq, n_pages = page_table.shape
    g_pages = PAGES_PER_STEP
    n_steps = n_pages // g_pages
    rows = 2 * N_HEADS

    def page_map(g, nd):
        def index(b, s, pt):
            return (layer, pt[b, n_pages - 1 - (s * g_pages + g)]) + (0,) * nd
        return index

    tok = lambda b, s, pt: (b, 0, 0)
    in_specs = [
        pl.BlockSpec((1, 1, D_ATTN), tok),
        pl.BlockSpec((1, 1, D_ATTN), tok),
        pl.BlockSpec((1, 1, D_ATTN), tok),
        pl.BlockSpec((1, 1, LANES), tok),
        pl.BlockSpec(sfx.shape, lambda b, s, pt: (0, 0)),
    ]
    page = (None, None, N_HEADS, HEAD_DIM, PAGE_SIZE)
    in_specs += [pl.BlockSpec(page, page_map(g, 3)) for g in range(g_pages)]
    in_specs += [pl.BlockSpec(page, page_map(g, 3)) for g in range(g_pages)]
    in_specs += [pl.BlockSpec((None, None, N_HEADS, PAGE_SIZE), page_map(g, 2)) for g in range(g_pages)]
    grid_spec = pltpu.PrefetchScalarGridSpec(
        num_scalar_prefetch=1,
        grid=(n_seq, n_steps),
        in_specs=in_specs,
        out_specs=pl.BlockSpec((1, 1, D_ATTN), tok),
        scratch_shapes=[
            pltpu.VMEM((rows, 1), F32),
            pltpu.VMEM((rows, 1), F32),
            pltpu.VMEM((rows, D_ATTN), F32),
            pltpu.VMEM((rows, LANES), F32),
        ],
    )
    return pl.pallas_call(
        functools.partial(_sample_attn_body, n_steps=n_steps),
        grid_spec=grid_spec,
        out_shape=jax.ShapeDtypeStruct((n_seq, 1, D_ATTN), F32),
        compiler_params=_cparams(("arbitrary", "arbitrary")),
        name="sample_attn",
    )(page_table, q_s, k_s, v_s, lf_s, sfx,
      *([cache_kt] * g_pages), *([cache_vt] * g_pages), *([cache_lft] * g_pages))


def _outproj_body(x_ref, po_ref, at_ref, wo_ref, g_ref, x1_ref, xn_ref):
    mix = jnp.concatenate([po_ref[...], at_ref[...]], axis=1).astype(BF16)
    x1 = x_ref[...] + _dot(mix, wo_ref[...])
    x1_ref[...] = x1
    ms = jnp.mean(x1 * x1, axis=-1, keepdims=True)
    xn_ref[...] = (x1 * lax.rsqrt(ms + EPS) * g_ref[...]).astype(BF16)


def _outproj(x, po, at, wo, g):
    n = x.shape[0]
    t = TOK_TILE
    row = lambda i: (i, 0)
    fix = lambda i: (0, 0)
    return pl.pallas_call(
        _outproj_body,
        grid=(n // t,),
        in_specs=[
            pl.BlockSpec((t, D_MODEL), row),
            pl.BlockSpec((t, D_POOL), row),
            pl.BlockSpec((t, D_ATTN), row),
            pl.BlockSpec(wo.shape, fix),
            pl.BlockSpec((1, D_MODEL), fix),
        ],
        out_specs=[pl.BlockSpec((t, D_MODEL), row), pl.BlockSpec((t, D_MODEL), row)],
        out_shape=[jax.ShapeDtypeStruct((n, D_MODEL), F32), jax.ShapeDtypeStruct((n, D_MODEL), BF16)],
        compiler_params=_cparams(("arbitrary",)),
        name="outproj",
    )(x, po, at, wo, g)


def _top16_ranks(s, sv_ref, half, rowi):
    rank = jnp.full(s.shape, float(PEER_TOPK), F32)
    for c in range(PEER_TOPK):
        m = jnp.max(s, axis=0, keepdims=True)
        first = jnp.min(jnp.where(s == m, rowi, float(N_KEYS)), axis=0, keepdims=True)
        sel = rowi == first
        rank = jnp.where(sel, float(c), rank)
        s = jnp.where(sel, -jnp.inf, s)
        sv_ref[half, c:c + 1, :] = m
    return rank


def _top16_ranks_distinct(s, sv_ref, half):
    rank = jnp.full(s.shape, float(PEER_TOPK), F32)
    for c in range(PEER_TOPK):
        m = jnp.max(s, axis=0, keepdims=True)
        sel = s == m
        rank = jnp.where(sel, float(c), rank)
        s = jnp.where(sel, -jnp.inf, s)
        sv_ref[half, c:c + 1, :] = m
    return rank


def _pair_segments():
    segs, row = [], 0
    for c in range(PEER_TOPK):
        nd = PEER_TOPK // (c + 1)
        if nd < 8:
            break
        segs.append((row, c, nd))
        row += nd
    bins = []
    for c in range(len(segs), PEER_TOPK):
        nd = PEER_TOPK // (c + 1)
        for bn in bins:
            if bn[0] + nd <= 8:
                bn[1].append((bn[0], c, nd))
                bn[0] += nd
                break
        else:
            bins.append([nd, [(0, c, nd)]])
    for bn in bins:
        segs += [(row + o, c, nd) for o, c, nd in bn[1]]
        row += 8
    return segs, row


def _peer_route_body(xn_ref, wq_ref, sk_ref, flat_ref, a_ref, n_ref, b_ref, r_ref,
                     q_scr, sv_ref, rk_ref, cand_scr, wgt_scr):
    t = xn_ref.shape[0]
    q = _dot(xn_ref[...], wq_ref[...]).astype(BF16)
    for hp in range(2 * PEER_HEADS):
        q_scr[hp] = q[:, hp * D_HALF:(hp + 1) * D_HALF]
    rowi = lax.broadcasted_iota(jnp.int32, (N_KEYS, t), 0).astype(F32)
    segs, n_rows = _pair_segments()
    flat = flat_ref[...]
    big = float(PEER_TOPK * PEER_TOPK)
    cand_scr[...] = jnp.full(cand_scr.shape, -jnp.inf, F32)
    wgt_scr[...] = jnp.zeros_like(wgt_scr)

    def head(h, _):
        s1 = _dot_nt(sk_ref[0], q_scr[2 * h])
        s2 = _dot_nt(sk_ref[1], q_scr[2 * h + 1])
        rk_ref[0] = _top16_ranks_distinct(s1, sv_ref, 0)
        rk_ref[1] = _top16_ranks_distinct(s2, sv_ref, 1)
        ranked = (jnp.sum(jnp.where(rk_ref[0] < float(PEER_TOPK), 1.0, 0.0), axis=0, keepdims=True)
                  + jnp.sum(jnp.where(rk_ref[1] < float(PEER_TOPK), 1.0, 0.0), axis=0, keepdims=True))

        @pl.when(jnp.max(jnp.abs(ranked - float(2 * PEER_TOPK))) > 0.0)
        def _():
            rk_ref[0] = _top16_ranks(s1, sv_ref, 0, rowi)
            rk_ref[1] = _top16_ranks(s2, sv_ref, 1, rowi)

        r1 = rk_ref[0]
        r2 = rk_ref[1]
        sv1 = sv_ref[0]
        sv2 = sv_ref[1]
        e1 = jnp.exp(sv1 - sv1[0:1, :])
        e2 = jnp.exp(sv2 - sv2[0:1, :])
        for row, c, nd in segs:
            cand_scr[row:row + nd, :] = sv1[c:c + 1, :] + sv2[0:nd, :]
            wgt_scr[row:row + nd, :] = e1[c:c + 1, :] * e2[0:nd, :]
        cand = cand_scr[...]
        chosen = jnp.zeros(cand.shape, F32)
        for _k in range(PEER_TOPK):
            m = jnp.max(cand, axis=0, keepdims=True)
            first = jnp.min(jnp.where(cand == m, flat, big), axis=0, keepdims=True)
            sel = flat == first
            chosen = jnp.where(sel, 1.0, chosen)
            cand = jnp.where(sel, -jnp.inf, cand)
        z = jnp.sum(chosen * wgt_scr[...], axis=0, keepdims=True)
        nrow = jnp.zeros((N_KEYS, t), F32)
        for row, c, nd in segs:
            n_c = jnp.sum(chosen[row:row + nd, :], axis=0, keepdims=True)
            nrow = jnp.where(r1 == float(c), n_c, nrow)
        a_ref[h] = jnp.where(r1 < float(PEER_TOPK), jnp.exp(s1 - sv1[0:1, :]) / z, 0.0)
        n_ref[h] = nrow
        b_ref[h] = jnp.exp(s2 - sv2[0:1, :]).astype(BF16)
        r_ref[h] = r2.astype(BF16)
        return 0

    lax.fori_loop(0, PEER_HEADS, head, 0)


def _peer_route(xn, wq, sk):
    n = xn.shape[0]
    t = SEQ_TILE
    fac = pl.BlockSpec((PEER_HEADS, N_KEYS, t), lambda i: (0, 0, i))
    shp = jax.ShapeDtypeStruct((PEER_HEADS, N_KEYS, n), F32)
    shp_bf = jax.ShapeDtypeStruct((PEER_HEADS, N_KEYS, n), BF16)
    segs, n_rows = _pair_segments()
    flat = np.full((n_rows, t), 1e9, np.float32)
    for row, c, nd in segs:
        flat[row:row + nd, :] = (c * PEER_TOPK + np.arange(nd, dtype=np.float32))[:, None]
    return pl.pallas_call(
        _peer_route_body,
        grid=(n // t,),
        in_specs=[
            pl.BlockSpec((t, D_MODEL), lambda i: (i, 0)),
            pl.BlockSpec(wq.shape, lambda i: (0, 0)),
            pl.BlockSpec(sk.shape, lambda i: (0, 0, 0)),
            pl.BlockSpec(flat.shape, lambda i: (0, 0)),
        ],
        out_specs=[fac, fac, fac, fac],
        out_shape=[shp, shp, shp_bf, shp_bf],
        scratch_shapes=[
            pltpu.VMEM((2 * PEER_HEADS, t, D_HALF), BF16),
            pltpu.VMEM((2, PEER_TOPK, t), F32),
            pltpu.VMEM((2, N_KEYS, t), F32),
            pltpu.VMEM((n_rows, t), F32),
            pltpu.VMEM((n_rows, t), F32),
        ],
        compiler_params=_cparams(("arbitrary",)),
        name="peer_route",
    )(xn, wq, sk, jnp.asarray(flat))


def _peer_expert_body(xn_ref, a_ref, n_ref, b_ref, r_ref, u_ref, vt_ref, x1_ref,
                      o_ref, acc_ref, act0_ref, act1_ref, *, n_blocks):
    s = pl.program_id(1)
    acts = (act0_ref, act1_ref)

    def project(slot):
        acts[slot][...] = _dot_nt(u_ref[...], xn_ref[...])

    def combine(slot):
        a_rows = a_ref[...].astype(BF16)
        n_rows = n_ref[...].astype(BF16)
        zero = jnp.zeros((), BF16)
        parts = []
        for i in range(EXP_ROWS):
            act = acts[slot][i * N_KEYS:(i + 1) * N_KEYS, :]
            ge = jax.nn.gelu(act.astype(BF16), approximate=True)
            gate = None
            for h in range(PEER_HEADS):
                keep = r_ref[h] < n_rows[h, i:i + 1, :]
                term = jnp.where(keep, b_ref[h], zero) * a_rows[h, i:i + 1, :]
                gate = term if gate is None else gate + term
            parts.append(gate * ge)
        w = jnp.concatenate(parts, axis=0)
        acc_ref[...] += _dot(vt_ref[...], w)

    first = s == 0
    last = s == n_blocks
    even = s % 2 == 0

    @pl.when(first)
    def _():
        acc_ref[...] = jnp.zeros_like(acc_ref)
        project(0)

    @pl.when(jnp.logical_and(jnp.logical_not(first), jnp.logical_and(jnp.logical_not(last), even)))
    def _():
        project(0)
        combine(1)

    @pl.when(jnp.logical_and(jnp.logical_not(last), jnp.logical_not(even)))
    def _():
        project(1)
        combine(0)

    @pl.when(last)
    def _():
        combine((n_blocks - 1) % 2)
        o_ref[...] = x1_ref[...] + acc_ref[...].T


def _peer_expert(xn, fa, fn, fb, fr, u, vt, x1):
    n = xn.shape[0]
    t = TOK_TILE
    eb = EXP_ROWS * N_KEYS
    n_blocks = u.shape[0] // eb
    tok = lambda i, s: (i, 0)
    proj = lambda s: jnp.minimum(s, n_blocks - 1)
    comb = lambda s: jnp.maximum(s - 1, 0)
    return pl.pallas_call(
        functools.partial(_peer_expert_body, n_blocks=n_blocks),
        grid=(n // t, n_blocks + 1),
        in_specs=[
            pl.BlockSpec((t, D_MODEL), tok),
            pl.BlockSpec((PEER_HEADS, EXP_ROWS, t), lambda i, s: (0, comb(s), i)),
            pl.BlockSpec((PEER_HEADS, EXP_ROWS, t), lambda i, s: (0, comb(s), i)),
            pl.BlockSpec((PEER_HEADS, N_KEYS, t), lambda i, s: (0, 0, i)),
            pl.BlockSpec((PEER_HEADS, N_KEYS, t), lambda i, s: (0, 0, i)),
            pl.BlockSpec((eb, D_MODEL), lambda i, s: (proj(s), 0)),
            pl.BlockSpec((D_MODEL, eb), lambda i, s: (0, comb(s))),
            pl.BlockSpec((t, D_MODEL), tok),
        ],
        out_specs=pl.BlockSpec((t, D_MODEL), tok),
        out_shape=jax.ShapeDtypeStruct((n, D_MODEL), F32),
        scratch_shapes=[pltpu.VMEM((D_MODEL, t), F32), pltpu.VMEM((eb, t), F32),
                        pltpu.VMEM((eb, t), F32)],
        compiler_params=_cparams(("arbitrary", "arbitrary")),
        name="peer_expert",
    )(xn, fa, fn, fb, fr, u, vt, x1)


def _selectors():
    wide = D_ATTN + 4 * LANES
    selq = np.zeros((N_HEADS, wide, AUG), np.float32)
    selk = np.zeros((N_HEADS, wide, AUG), np.float32)
    selv = np.zeros((N_HEADS, AUG, D_ATTN + LANES), np.float32)
    ones_row = D_ATTN + 3 * LANES
    for h in range(N_HEADS):
        for d in range(HEAD_DIM):
            selq[h, h * HEAD_DIM + d, d] = ATTN_SCALE
            selk[h, h * HEAD_DIM + d, d] = 1.0
            selv[h, d, h * HEAD_DIM + d] = 1.0
        for piece in range(3):
            selq[h, D_ATTN + piece * LANES + h, HEAD_DIM + piece] = 1.0
            selq[h, ones_row, HEAD_DIM + 3 + piece] = 1.0
            selk[h, ones_row, HEAD_DIM + piece] = 1.0
            selk[h, D_ATTN + piece * LANES + h, HEAD_DIM + 3 + piece] = -1.0
        selv[h, HEAD_DIM, D_ATTN] = 1.0
    return (jnp.asarray(selq, BF16), jnp.asarray(selk, BF16), jnp.asarray(selv, BF16))


def kernel(x_prompt, x_sample, cache_k, cache_v, cache_logf, state_pool, page_table,
           norm_mix, w_in, b_forget, q_norm, k_norm, pool_w, pool_scale, w_out,
           norm_ffn, peer_wq, peer_subkeys, peer_u, peer_v):
    batch, seq, _ = x_prompt.shape
    n_seq, dec_seq, _ = x_sample.shape
    depth = w_in.shape[0]
    n_pool = cache_k.shape[1]
    n_pages = page_table.shape[1]
    past = n_pages * PAGE_SIZE
    assert dec_seq == 1 and seq % KV_TILE == 0 and n_pages % PAGES_PER_STEP == 0
    assert n_seq % 8 == 0
    n_prompt = batch * seq
    n_tok = n_prompt + n_seq
    n_all = -(-n_tok // TOK_TILE) * TOK_TILE

    x = jnp.concatenate([x_prompt.reshape(n_prompt, D_MODEL), x_sample.reshape(n_seq, D_MODEL),
                         jnp.zeros((n_all - n_tok, D_MODEL), F32)], axis=0)
    selq, selk, selv = _selectors()
    hm = jnp.asarray(np.kron(np.eye(N_HEADS), np.full((HEAD_DIM, HEAD_DIM), 1.0 / HEAD_DIM)), BF16)
    jj = np.arange(PAGE_SIZE)
    sfx = jnp.asarray(np.concatenate([(jj[:, None] > jj[None, :]).astype(np.float32),
                                      np.ones((PAGE_SIZE, PAGE_SIZE), np.float32)], axis=1), BF16)
    cache_kt = jnp.transpose(cache_k, (0, 1, 3, 4, 2))
    cache_vt = jnp.transpose(cache_v, (0, 1, 3, 4, 2))
    cache_lft = jnp.transpose(cache_logf, (0, 1, 3, 2))

    kp_l, vp_l, fp_l, pp_l, ks_l, vs_l, fs_l, ps_l = [], [], [], [], [], [], [], []
    for l in range(depth):
        wm = w_in[l][:, :D_POOL + 3 * D_ATTN].astype(BF16)
        wf = jnp.pad(w_in[l][:, D_POOL + 3 * D_ATTN:], ((0, 0), (0, LANES - N_HEADS))).astype(BF16)
        bfp = jnp.pad(b_forget[l], (0, LANES - N_HEADS)).reshape(1, LANES)
        xp, q, k, v, lf = _inproj(x, norm_mix[l].reshape(1, D_MODEL), wm, wf, bfp,
                                  jnp.tile(q_norm[l], N_HEADS).reshape(1, D_ATTN),
                                  jnp.tile(k_norm[l], N_HEADS).reshape(1, D_ATTN), hm)
        pw = pool_w[l].astype(BF16)
        ps = pool_scale[l].reshape(1, D_POOL)

        po_p, qa, ka, vt = _poolprep(xp, lf, q, k, v, pw, ps, selq, selk, selv, batch, seq)
        at_p = _flash(qa, ka, vt, batch, seq)

        sl = slice(n_prompt, n_tok)
        xp_s = xp[sl]
        po_s = _sample_pool(xp_s, jnp.transpose(state_pool[l], (1, 0, 2)), pw, ps, past)
        row = lambda t: t[sl].reshape(n_seq, 1, t.shape[1])
        at_s = _sample_attn(page_table, row(q), row(k), row(v), row(lf), sfx,
                            cache_kt, cache_vt, cache_lft, l).reshape(n_seq, D_ATTN)

        pad = jnp.zeros((n_all - n_tok, D_POOL), F32)
        po = jnp.concatenate([po_p, po_s, pad], axis=0)
        at = jnp.concatenate([at_p, at_s, pad], axis=0)
        x1, xn = _outproj(x, po, at, w_out[l].astype(BF16), norm_ffn[l].reshape(1, D_MODEL))
        fa, fn, fb, fr = _peer_route(xn, peer_wq[l].astype(BF16), peer_subkeys[l].astype(BF16))
        x = _peer_expert(xn, fa, fn, fb, fr, peer_u[l].astype(BF16),
                         jnp.transpose(peer_v[l]).astype(BF16), x1)

        kp_l.append(k[:n_prompt].reshape(batch, seq, N_HEADS, HEAD_DIM))
        vp_l.append(v[:n_prompt].reshape(batch, seq, N_HEADS, HEAD_DIM))
        fp_l.append(lf[:n_prompt, :N_HEADS].reshape(batch, seq, N_HEADS))
        pp_l.append(xp[:n_prompt].reshape(batch, seq, D_POOL)[:, seq - POOL_STATE:])
        ks_l.append(k[sl].reshape(n_seq, 1, N_HEADS, HEAD_DIM))
        vs_l.append(v[sl].reshape(n_seq, 1, N_HEADS, HEAD_DIM))
        fs_l.append(lf[sl, :N_HEADS].reshape(n_seq, 1, N_HEADS))
        ps_l.append(jnp.concatenate([state_pool[l][:, 1:], xp_s[:, None, :]], axis=1))

    return (x[:n_prompt].reshape(batch, seq, D_MODEL), x[sl].reshape(n_seq, 1, D_MODEL),
            jnp.stack(kp_l), jnp.stack(vp_l), jnp.stack(fp_l), jnp.stack(pp_l),
            jnp.stack(ks_l), jnp.stack(vs_l), jnp.stack(fs_l), jnp.stack(ps_l))
```

```python
import functools

import jax
import jax.numpy as jnp
import numpy as np
from jax import lax
from jax.experimental import pallas as pl
from jax.experimental.pallas import tpu as pltpu

F32 = jnp.float32
BF16 = jnp.bfloat16

EPS = 1e-6
D_MODEL = 1024
D_POOL = 512
POOL_WINDOWS = (2, 4, 8, 16)
POOL_GROUP = 128
POOL_STATE = 15
N_HEADS = 8
HEAD_DIM = 64
D_ATTN = N_HEADS * HEAD_DIM
PAGE_SIZE = 128
PEER_HEADS = 8
PEER_TOPK = 16
N_KEYS = 128
D_HALF = 128
ATTN_SCALE = HEAD_DIM ** -0.5

LANES = 128
AUG = 128
V_ROWS = 80
TOK_TILE = 512
SEQ_TILE = 512
ROUTE_TILE = 256
Q_TILE = 256
KV_TILE = 512
EXP_ROWS = 8
PAGES_PER_STEP = 16
VMEM_LIMIT = 56 * 1024 * 1024
NEG = -1e30


def _cparams(sem):
    return pltpu.CompilerParams(dimension_semantics=sem, vmem_limit_bytes=VMEM_LIMIT)


def _split3(x):
    hi = x.astype(BF16)
    r = x - hi.astype(F32)
    mid = r.astype(BF16)
    lo = (r - mid.astype(F32)).astype(BF16)
    return hi, mid, lo


def _dot(a, b):
    return jnp.dot(a, b, preferred_element_type=F32)


def _dot_nt(a, b):
    return lax.dot_general(a, b, (((1,), (1,)), ((), ())), preferred_element_type=F32)


def _inproj_body(x_ref, g_ref, wm_ref, wf_ref, bf_ref, qg_ref, kg_ref, hm_ref,
                 xp_ref, q_ref, k_ref, v_ref, lf_ref):
    x = x_ref[...]
    ms = jnp.mean(x * x, axis=-1, keepdims=True)
    h = (x * lax.rsqrt(ms + EPS) * g_ref[...]).astype(BF16)
    z = _dot(h, wm_ref[...])
    xp_ref[...] = z[:, 0:D_POOL]
    hm = hm_ref[...]

    def headnorm(t, gain):
        sq = t * t
        hi = sq.astype(BF16)
        lo = (sq - hi.astype(F32)).astype(BF16)
        msh = _dot(hi, hm) + _dot(lo, hm)
        return t * lax.rsqrt(msh + EPS) * gain

    q_ref[...] = headnorm(z[:, D_POOL:D_POOL + D_ATTN], qg_ref[...])
    k_ref[...] = headnorm(z[:, D_POOL + D_ATTN:D_POOL + 2 * D_ATTN], kg_ref[...])
    v_ref[...] = z[:, D_POOL + 2 * D_ATTN:D_POOL + 3 * D_ATTN]
    f = _dot(h, wf_ref[...]) + bf_ref[...]
    lf_ref[...] = jnp.minimum(f, 0.0) - jnp.log1p(jnp.exp(-jnp.abs(f)))


def _inproj(x, g, wm, wf, bfp, qg, kg, hm):
    n = x.shape[0]
    t = TOK_TILE
    row = lambda i: (i, 0)
    fix = lambda i: (0, 0)
    return pl.pallas_call(
        _inproj_body,
        grid=(n // t,),
        in_specs=[
            pl.BlockSpec((t, D_MODEL), row),
            pl.BlockSpec((1, D_MODEL), fix),
            pl.BlockSpec(wm.shape, fix),
            pl.BlockSpec(wf.shape, fix),
            pl.BlockSpec((1, LANES), fix),
            pl.BlockSpec((1, D_ATTN), fix),
            pl.BlockSpec((1, D_ATTN), fix),
            pl.BlockSpec(hm.shape, fix),
        ],
        out_specs=[
            pl.BlockSpec((t, D_POOL), row),
            pl.BlockSpec((t, D_ATTN), row),
            pl.BlockSpec((t, D_ATTN), row),
            pl.BlockSpec((t, D_ATTN), row),
            pl.BlockSpec((t, LANES), row),
        ],
        out_shape=[
            jax.ShapeDtypeStruct((n, D_POOL), F32),
            jax.ShapeDtypeStruct((n, D_ATTN), F32),
            jax.ShapeDtypeStruct((n, D_ATTN), F32),
            jax.ShapeDtypeStruct((n, D_ATTN), F32),
            jax.ShapeDtypeStruct((n, LANES), F32),
        ],
        compiler_params=_cparams(("arbitrary",)),
        name="inproj",
    )(x, g, wm, wf, bfp, qg, kg, hm)


def _pool_mix(xp, window_sum, cnt_fn, pw_ref, ps_ref):
    outs = []
    for g, w in enumerate(POOL_WINDOWS):
        lanes = slice(g * POOL_GROUP, (g + 1) * POOL_GROUP)
        pooled = window_sum(g, w) / cnt_fn(w) - xp[:, lanes]
        outs.append(_dot(pooled.astype(BF16), pw_ref[g]))
    return jnp.concatenate(outs, axis=1) * ps_ref[...]


def _poolprep_body(xp_ref, lf_ref, q_ref, k_ref, v_ref, pw_ref, ps_ref,
                   selq_ref, selk_ref, selv_ref,
                   po_ref, qa_ref, ka_ref, vt_ref, xx_ref, fc_ref):
    t = SEQ_TILE
    hist = 16
    step = pl.program_id(1)

    @pl.when(step == 0)
    def _():
        xx_ref[0:hist, :] = jnp.zeros((hist, D_POOL), F32)
        fc_ref[...] = jnp.zeros_like(fc_ref)

    xp = xp_ref[...]
    xx_ref[hist:hist + t, :] = xp
    pos = lax.broadcasted_iota(jnp.int32, (t, POOL_GROUP), 0) + step * t

    def window_sum(g, w):
        lanes = slice(g * POOL_GROUP, (g + 1) * POOL_GROUP)
        ws = xp[:, lanes]
        for r in range(1, w):
            ws = ws + xx_ref[hist - r:hist - r + t, lanes]
        return ws

    def cnt(w):
        return jnp.minimum(pos + 1, w).astype(F32)

    po_ref[...] = _pool_mix(xp, window_sum, cnt, pw_ref, ps_ref)
    xx_ref[0:hist, :] = xx_ref[t:t + hist, :]

    ri = lax.broadcasted_iota(jnp.int32, (t, t), 0)
    ci = lax.broadcasted_iota(jnp.int32, (t, t), 1)
    tri = jnp.where(ci <= ri, 1.0, 0.0).astype(BF16)
    hi, mid, lo = _split3(lf_ref[...])
    fcum = _dot(tri, hi) + _dot(tri, mid) + _dot(tri, lo) + fc_ref[...]
    fc_ref[...] = fcum[t - 1:t, :]
    fh, fm, fl = _split3(fcum)
    ones = jnp.ones((t, LANES), BF16)
    wq = jnp.concatenate([q_ref[...].astype(BF16), fh, fm, fl, ones], axis=1)
    wk = jnp.concatenate([k_ref[...].astype(BF16), fh, fm, fl, ones], axis=1)
    wv = jnp.concatenate([v_ref[...].astype(BF16), ones], axis=1)
    for h in range(N_HEADS):
        qa_ref[h] = _dot(wq, selq_ref[h]).astype(BF16)
        ka_ref[h] = _dot(wk, selk_ref[h]).astype(BF16)
        vt_ref[h] = _dot_nt(selv_ref[h], wv).astype(BF16)


def _poolprep(xp, lf, q, k, v, pw, ps, selq, selk, selv, batch, seq):
    t = SEQ_TILE
    nt = seq // t
    n = batch * seq
    row = lambda b, i: (b * nt + i, 0)
    fix2 = lambda b, i: (0, 0)
    fix3 = lambda b, i: (0, 0, 0)
    return pl.pallas_call(
        _poolprep_body,
        grid=(batch, nt),
        in_specs=[
            pl.BlockSpec((t, D_POOL), row),
            pl.BlockSpec((t, LANES), row),
            pl.BlockSpec((t, D_ATTN), row),
            pl.BlockSpec((t, D_ATTN), row),
            pl.BlockSpec((t, D_ATTN), row),
            pl.BlockSpec(pw.shape, fix3),
            pl.BlockSpec((1, D_POOL), fix2),
            pl.BlockSpec(selq.shape, fix3),
            pl.BlockSpec(selk.shape, fix3),
            pl.BlockSpec(selv.shape, fix3),
        ],
        out_specs=[
            pl.BlockSpec((t, D_POOL), row),
            pl.BlockSpec((N_HEADS, t, AUG), lambda b, i: (0, b * nt + i, 0)),
            pl.BlockSpec((N_HEADS, t, AUG), lambda b, i: (0, b * nt + i, 0)),
            pl.BlockSpec((N_HEADS, None, AUG, t), lambda b, i: (0, b * nt + i, 0, 0)),
        ],
        out_shape=[
            jax.ShapeDtypeStruct((n, D_POOL), F32),
            jax.ShapeDtypeStruct((N_HEADS, n, AUG), BF16),
            jax.ShapeDtypeStruct((N_HEADS, n, AUG), BF16),
            jax.ShapeDtypeStruct((N_HEADS, n // t, AUG, t), BF16),
        ],
        scratch_shapes=[
            pltpu.VMEM((t + 16, D_POOL), F32),
            pltpu.VMEM((1, LANES), F32),
        ],
        compiler_params=_cparams(("arbitrary", "arbitrary")),
        name="poolprep",
    )(xp, lf, q, k, v, pw, ps, selq, selk, selv)


def _flash_body(qa_ref, ka_ref, vt_ref, o_ref, s0_scr, s1_scr):
    tq, tk = Q_TILE, KV_TILE
    i = pl.program_id(2)
    n_full = (i * tq) // tk
    qpos = lax.broadcasted_iota(jnp.int32, (tk, tq), 1) + i * tq
    krel = lax.broadcasted_iota(jnp.int32, (tk, tq), 0)
    vrows = V_ROWS

    s_slots = (s0_scr, s1_scr)

    def scores(j, slot):
        start = pl.multiple_of(j * tk, tk)
        for hh in range(2):
            s_slots[slot][hh] = _dot_nt(ka_ref[hh, pl.ds(start, tk), :], qa_ref[hh])

    def tile(j, slot, carry, masked):
        if not masked:
            scores(j + 1, 1 - slot)
        new = []
        for hh in range(2):
            m, acc = carry[hh]
            s = s_slots[slot][hh]
            if masked:
                s = jnp.where(krel + j * tk <= qpos, s, NEG)
            m_new = jnp.maximum(m, jnp.max(s, axis=0, keepdims=True))
            p = jnp.exp(s - m_new).astype(BF16)
            alpha = jnp.exp(m - m_new)
            acc = alpha * acc + _dot(vt_ref[hh, j, 0:vrows, :], p)
            new.append((m_new, acc))
        return tuple(new)

    def pair(jj, carry):
        carry = tile(2 * jj, 0, carry, False)
        return tile(2 * jj + 1, 1, carry, False)

    scores(0, 0)
    one = (jnp.full((1, tq), NEG, F32), jnp.zeros((vrows, tq), F32))
    carry = lax.fori_loop(0, n_full // 2, pair, (one, one))
    j0 = 2 * (n_full // 2)
    carry = lax.cond(
        n_full % 2 == 1,
        lambda c: tile(j0 + 1, 1, tile(j0, 0, c, False), True),
        lambda c: tile(j0, 0, c, True),
        carry)
    outs = []
    for hh in range(2):
        acc = jnp.concatenate([carry[hh][1], jnp.zeros((AUG - vrows, tq), F32)], axis=0)
        acc_t = acc.T
        outs.append(acc_t[:, 0:HEAD_DIM] / acc_t[:, HEAD_DIM:HEAD_DIM + 1])
    o_ref[...] = jnp.concatenate(outs, axis=1)


def _flash(qa, ka, vt, batch, seq):
    tq = Q_TILE
    nq = seq // tq
    n = batch * seq
    return pl.pallas_call(
        _flash_body,
        grid=(batch, N_HEADS // 2, nq),
        in_specs=[
            pl.BlockSpec((2, tq, AUG), lambda b, hp, i: (hp, b * nq + i, 0)),
            pl.BlockSpec((2, seq, AUG), lambda b, hp, i: (hp, b, 0)),
            pl.BlockSpec((2, seq // KV_TILE, AUG, KV_TILE), lambda b, hp, i: (hp, b, 0, 0)),
        ],
        out_specs=pl.BlockSpec((None, tq, 2 * HEAD_DIM), lambda b, hp, i: (hp, b * nq + i, 0)),
        out_shape=jax.ShapeDtypeStruct((N_HEADS // 2, n, 2 * HEAD_DIM), F32),
        scratch_shapes=[pltpu.VMEM((2, KV_TILE, Q_TILE), F32), pltpu.VMEM((2, KV_TILE, Q_TILE), F32)],
        compiler_params=_cparams(("arbitrary", "arbitrary", "arbitrary")),
        name="flash",
    )(qa, ka, vt)


def _sample_pool_body(xp_ref, st_ref, pw_ref, ps_ref, o_ref, *, start):
    xp = xp_ref[...]

    def window_sum(g, w):
        lanes = slice(g * POOL_GROUP, (g + 1) * POOL_GROUP)
        ws = xp[:, lanes]
        for r in range(1, w):
            ws = ws + st_ref[POOL_STATE - r][:, lanes]
        return ws

    o_ref[...] = _pool_mix(xp, window_sum, lambda w: float(min(start + 1, w)), pw_ref, ps_ref)


def _sample_pool(xp_s, state_t, pw, ps, start):
    return pl.pallas_call(
        functools.partial(_sample_pool_body, start=start),
        out_shape=jax.ShapeDtypeStruct(xp_s.shape, F32),
        compiler_params=pltpu.CompilerParams(vmem_limit_bytes=VMEM_LIMIT),
        name="sample_pool",
    )(xp_s, state_t, pw, ps)


def _sample_attn_body(pt_ref, q_ref, k_ref, v_ref, lf_ref, sfx_ref, *refs, n_steps):
    g_pages = PAGES_PER_STEP
    ck = refs[0:g_pages]
    cv = refs[g_pages:2 * g_pages]
    cl = refs[2 * g_pages:3 * g_pages]
    o_ref = refs[3 * g_pages]
    m_ref, l_ref, acc_ref, car_ref = refs[3 * g_pages + 1:]
    rows = 2 * N_HEADS
    step = pl.program_id(1)

    hrow = lax.broadcasted_iota(jnp.int32, (rows, D_ATTN), 0)
    hcol = lax.broadcasted_iota(jnp.int32, (rows, D_ATTN), 1) // HEAD_DIM
    headmask = hrow == hcol
    qf = q_ref[0].astype(BF16).astype(F32)
    qbd = jnp.where(headmask, qf * ATTN_SCALE, 0.0)

    @pl.when(step == 0)
    def _():
        m_ref[...] = jnp.full(m_ref.shape, NEG, F32)
        l_ref[...] = jnp.zeros_like(l_ref)
        acc_ref[...] = jnp.zeros_like(acc_ref)
        eye = (lax.broadcasted_iota(jnp.int32, (rows, LANES), 0)
               == lax.broadcasted_iota(jnp.int32, (rows, LANES), 1))
        c_new = jnp.sum(jnp.where(eye, lf_ref[0], 0.0), axis=1, keepdims=True)
        car_ref[...] = jnp.broadcast_to(c_new, car_ref.shape)

    carry = car_ref[...]
    sfx = sfx_ref[...]
    bias = []
    for g in range(g_pages):
        lft = cl[g][...]
        hi, mid, lo = _split3(jnp.concatenate([lft, jnp.zeros_like(lft)], axis=0))
        r = _dot(hi, sfx) + _dot(mid, sfx) + _dot(lo, sfx)
        bias.append(r[:, 0:PAGE_SIZE] + carry)
        carry = carry + r[:, PAGE_SIZE:2 * PAGE_SIZE]
    car_ref[...] = carry
    kt_all = jnp.concatenate(
        [ck[g][...].reshape(D_ATTN, PAGE_SIZE).astype(BF16) for g in range(g_pages)], axis=1)
    vt_all = jnp.concatenate(
        [cv[g][...].reshape(D_ATTN, PAGE_SIZE).astype(BF16) for g in range(g_pages)], axis=1)
    s_all = _dot(qbd.astype(BF16), kt_all) + jnp.concatenate(bias, axis=1)
    m_old = m_ref[...]
    m_new = jnp.maximum(m_old, jnp.max(s_all, axis=1, keepdims=True))
    p = jnp.exp(s_all - m_new)
    alpha = jnp.exp(m_old - m_new)
    l_ref[...] = alpha * l_ref[...] + jnp.sum(p, axis=1, keepdims=True)
    acc_ref[...] = alpha * acc_ref[...] + _dot_nt(p.astype(BF16), vt_all)
    m_ref[...] = m_new

    @pl.when(step == n_steps - 1)
    def _():
        kf = k_ref[0].astype(BF16).astype(F32)
        vf = v_ref[0].astype(BF16).astype(F32)
        s_new = jnp.sum(qbd * kf, axis=1, keepdims=True)
        m_old = m_ref[...]
        m_fin = jnp.maximum(m_old, s_new)
        a = jnp.exp(m_old - m_fin)
        p_new = jnp.exp(s_new - m_fin)
        l_fin = a * l_ref[...] + p_new
        acc = a * acc_ref[...] + p_new.astype(BF16).astype(F32) * vf
        o_ref[0] = jnp.sum(jnp.where(headmask, acc / l_fin, 0.0), axis=0, keepdims=True)


def _sample_attn(page_table, q_s, k_s, v_s, lf_s, sfx, cache_kt, cache_vt, cache_lft, layer):
    n_seq, n_pages = page_table.shape
    g_pages = PAGES_PER_STEP
    n_steps = n_pages // g_pages
    rows = 2 * N_HEADS

    def page_map(g, nd):
        def index(b, s, pt):
            return (layer, pt[b, n_pages - 1 - (s * g_pages + g)]) + (0,) * nd
        return index

    tok = lambda b, s, pt: (b, 0, 0)
    in_specs = [
        pl.BlockSpec((1, 1, D_ATTN), tok),
        pl.BlockSpec((1, 1, D_ATTN), tok),
        pl.BlockSpec((1, 1, D_ATTN), tok),
        pl.BlockSpec((1, 1, LANES), tok),
        pl.BlockSpec(sfx.shape, lambda b, s, pt: (0, 0)),
    ]
    page = (None, None, N_HEADS, HEAD_DIM, PAGE_SIZE)
    in_specs += [pl.BlockSpec(page, page_map(g, 3)) for g in range(g_pages)]
    in_specs += [pl.BlockSpec(page, page_map(g, 3)) for g in range(g_pages)]
    in_specs += [pl.BlockSpec((None, None, N_HEADS, PAGE_SIZE), page_map(g, 2)) for g in range(g_pages)]
    grid_spec = pltpu.PrefetchScalarGridSpec(
        num_scalar_prefetch=1,
        grid=(n_seq, n_steps),
        in_specs=in_specs,
        out_specs=pl.BlockSpec((1, 1, D_ATTN), tok),
        scratch_shapes=[
            pltpu.VMEM((rows, 1), F32),
            pltpu.VMEM((rows, 1), F32),
            pltpu.VMEM((rows, D_ATTN), F32),
            pltpu.VMEM((rows, LANES), F32),
        ],
    )
    return pl.pallas_call(
        functools.partial(_sample_attn_body, n_steps=n_steps),
        grid_spec=grid_spec,
        out_shape=jax.ShapeDtypeStruct((n_seq, 1, D_ATTN), F32),
        compiler_params=_cparams(("arbitrary", "arbitrary")),
        name="sample_attn",
    )(page_table, q_s, k_s, v_s, lf_s, sfx,
      *([cache_kt] * g_pages), *([cache_vt] * g_pages), *([cache_lft] * g_pages))


def _outproj_body(x_ref, po_ref, at_ref, wo_ref, g_ref, x1_ref, xn_ref):
    mix = jnp.concatenate([po_ref[...]] + [at_ref[hp] for hp in range(N_HEADS // 2)],
                          axis=1).astype(BF16)
    x1 = x_ref[...] + _dot(mix, wo_ref[...])
    x1_ref[...] = x1
    ms = jnp.mean(x1 * x1, axis=-1, keepdims=True)
    xn_ref[...] = (x1 * lax.rsqrt(ms + EPS) * g_ref[...]).astype(BF16)


def _outproj(x, po, at, wo, g):
    n = x.shape[0]
    t = TOK_TILE
    row = lambda i: (i, 0)
    fix = lambda i: (0, 0)
    return pl.pallas_call(
        _outproj_body,
        grid=(n // t,),
        in_specs=[
            pl.BlockSpec((t, D_MODEL), row),
            pl.BlockSpec((t, D_POOL), row),
            pl.BlockSpec((N_HEADS // 2, t, 2 * HEAD_DIM), lambda i: (0, i, 0)),
            pl.BlockSpec(wo.shape, fix),
            pl.BlockSpec((1, D_MODEL), fix),
        ],
        out_specs=[pl.BlockSpec((t, D_MODEL), row), pl.BlockSpec((t, D_MODEL), row)],
        out_shape=[jax.ShapeDtypeStruct((n, D_MODEL), F32), jax.ShapeDtypeStruct((n, D_MODEL), BF16)],
        compiler_params=_cparams(("arbitrary",)),
        name="outproj",
    )(x, po, at, wo, g)


def _top16_ranks(s, sv_ref, half, rowi):
    rank = jnp.full(s.shape, float(PEER_TOPK), F32)
    for c in range(PEER_TOPK):
        m = jnp.max(s, axis=0, keepdims=True)
        first = jnp.min(jnp.where(s == m, rowi, float(N_KEYS)), axis=0, keepdims=True)
        sel = rowi == first
        rank = jnp.where(sel, float(c), rank)
        s = jnp.where(sel, -jnp.inf, s)
        sv_ref[half, c:c + 1, :] = m
    return rank


def _top16_ranks_distinct(s, sv_ref, half):
    rank = jnp.full(s.shape, float(PEER_TOPK), F32)
    for c in range(PEER_TOPK):
        m = jnp.max(s, axis=0, keepdims=True)
        sel = s == m
        rank = jnp.where(sel, float(c), rank)
        s = jnp.where(sel, -jnp.inf, s)
        sv_ref[half, c:c + 1, :] = m
    return rank


def _pair_segments():
    segs, row = [], 0
    for c in range(PEER_TOPK):
        nd = PEER_TOPK // (c + 1)
        if nd < 8:
            break
        segs.append((row, c, nd))
        row += nd
    bins = []
    for c in range(len(segs), PEER_TOPK):
        nd = PEER_TOPK // (c + 1)
        for bn in bins:
            if bn[0] + nd <= 8:
                bn[1].append((bn[0], c, nd))
                bn[0] += nd
                break
        else:
            bins.append([nd, [(0, c, nd)]])
    for bn in bins:
        segs += [(row + o, c, nd) for o, c, nd in bn[1]]
        row += 8
    return segs, row


def _peer_route_body(xn_ref, wq_ref, sk_ref, flat_ref, a_ref, n_ref, b_ref, r_ref,
                     q_scr, sv_ref, rk_ref, cand_scr, wgt_scr):
    t = xn_ref.shape[0]
    q = _dot(xn_ref[...], wq_ref[...]).astype(BF16)
    for hp in range(2 * PEER_HEADS):
        q_scr[hp] = q[:, hp * D_HALF:(hp + 1) * D_HALF]
    rowi = lax.broadcasted_iota(jnp.int32, (N_KEYS, t), 0).astype(F32)
    segs, n_rows = _pair_segments()
    flat = flat_ref[...]
    big = float(PEER_TOPK * PEER_TOPK)
    cand_scr[...] = jnp.full(cand_scr.shape, -jnp.inf, F32)
    wgt_scr[...] = jnp.zeros_like(wgt_scr)

    def head(h, _):
        s1 = _dot_nt(sk_ref[0], q_scr[2 * h])
        s2 = _dot_nt(sk_ref[1], q_scr[2 * h + 1])
        rk_ref[0] = _top16_ranks_distinct(s1, sv_ref, 0)
        rk_ref[1] = _top16_ranks_distinct(s2, sv_ref, 1)
        ranked = (jnp.sum(jnp.where(rk_ref[0] < float(PEER_TOPK), 1.0, 0.0), axis=0, keepdims=True)
                  + jnp.sum(jnp.where(rk_ref[1] < float(PEER_TOPK), 1.0, 0.0), axis=0, keepdims=True))

        @pl.when(jnp.max(jnp.abs(ranked - float(2 * PEER_TOPK))) > 0.0)
        def _():
            rk_ref[0] = _top16_ranks(s1, sv_ref, 0, rowi)
            rk_ref[1] = _top16_ranks(s2, sv_ref, 1, rowi)

        r1 = rk_ref[0]
        r2 = rk_ref[1]
        sv1 = sv_ref[0]
        sv2 = sv_ref[1]
        e1 = jnp.exp(sv1 - sv1[0:1, :])
        e2 = jnp.exp(sv2 - sv2[0:1, :])
        for row, c, nd in segs:
            cand_scr[row:row + nd, :] = sv1[c:c + 1, :] + sv2[0:nd, :]
            wgt_scr[row:row + nd, :] = e1[c:c + 1, :] * e2[0:nd, :]
        cand = cand_scr[...]
        chosen = jnp.zeros(cand.shape, F32)
        for _k in range(PEER_TOPK):
            m = jnp.max(cand, axis=0, keepdims=True)
            first = jnp.min(jnp.where(cand == m, flat, big), axis=0, keepdims=True)
            sel = flat == first
            chosen = jnp.where(sel, 1.0, chosen)
            cand = jnp.where(sel, -jnp.inf, cand)
        z = jnp.sum(chosen * wgt_scr[...], axis=0, keepdims=True)
        nrow = jnp.zeros((N_KEYS, t), F32)
        for row, c, nd in segs:
            n_c = jnp.sum(chosen[row:row + nd, :], axis=0, keepdims=True)
            nrow = jnp.where(r1 == float(c), n_c, nrow)
        a_ref[h] = jnp.where(r1 < float(PEER_TOPK), jnp.exp(s1 - sv1[0:1, :]) / z, 0.0)
        n_ref[h] = nrow
        b_ref[h] = jnp.exp(s2 - sv2[0:1, :]).astype(BF16)
        r_ref[h] = r2.astype(BF16)
        return 0

    lax.fori_loop(0, PEER_HEADS, head, 0)


def _peer_route(xn, wq, sk):
    n = xn.shape[0]
    t = ROUTE_TILE
    fac = pl.BlockSpec((None, PEER_HEADS, N_KEYS, t), lambda i: (i, 0, 0, 0))
    shp = jax.ShapeDtypeStruct((n // t, PEER_HEADS, N_KEYS, t), F32)
    shp_bf = jax.ShapeDtypeStruct((n // t, PEER_HEADS, N_KEYS, t), BF16)
    segs, n_rows = _pair_segments()
    flat = np.full((n_rows, t), 1e9, np.float32)
    for row, c, nd in segs:
        flat[row:row + nd, :] = (c * PEER_TOPK + np.arange(nd, dtype=np.float32))[:, None]
    return pl.pallas_call(
        _peer_route_body,
        grid=(n // t,),
        in_specs=[
            pl.BlockSpec((t, D_MODEL), lambda i: (i, 0)),
            pl.BlockSpec(wq.shape, lambda i: (0, 0)),
            pl.BlockSpec(sk.shape, lambda i: (0, 0, 0)),
            pl.BlockSpec(flat.shape, lambda i: (0, 0)),
        ],
        out_specs=[fac, fac, fac, fac],
        out_shape=[shp, shp, shp_bf, shp_bf],
        scratch_shapes=[
            pltpu.VMEM((2 * PEER_HEADS, t, D_HALF), BF16),
            pltpu.VMEM((2, PEER_TOPK, t), F32),
            pltpu.VMEM((2, N_KEYS, t), F32),
            pltpu.VMEM((n_rows, t), F32),
            pltpu.VMEM((n_rows, t), F32),
        ],
        compiler_params=_cparams(("arbitrary",)),
        name="peer_route",
    )(xn, wq, sk, jnp.asarray(flat))


def _peer_expert_body(xn_ref, a_ref, n_ref, b_ref, r_ref, u_ref, vt_ref, x1_ref,
                      o_ref, acc_ref, act0_ref, act1_ref, *, n_blocks):
    s = pl.program_id(1)
    acts = (act0_ref, act1_ref)

    def project(slot):
        acts[slot][...] = _dot_nt(u_ref[...], xn_ref[...])

    def combine(slot):
        a_rows = a_ref[...].astype(BF16)
        n_rows = n_ref[...].astype(BF16)
        zero = jnp.zeros((), BF16)
        rtile = a_ref.shape[-1]
        parts = []
        for i in range(EXP_ROWS):
            row = []
            for k in range(a_ref.shape[0]):
                act = acts[slot][i * N_KEYS:(i + 1) * N_KEYS, k * rtile:(k + 1) * rtile]
                ge = jax.nn.gelu(act.astype(BF16), approximate=True)
                gate = None
                for h in range(PEER_HEADS):
                    keep = r_ref[k, h] < n_rows[k, h, i:i + 1, :]
                    term = jnp.where(keep, b_ref[k, h], zero) * a_rows[k, h, i:i + 1, :]
                    gate = term if gate is None else gate + term
                row.append(gate * ge)
            parts.append(jnp.concatenate(row, axis=1))
        w = jnp.concatenate(parts, axis=0)
        acc_ref[...] += _dot(vt_ref[...], w)

    first = s == 0
    last = s == n_blocks
    even = s % 2 == 0

    @pl.when(first)
    def _():
        acc_ref[...] = jnp.zeros_like(acc_ref)
        project(0)

    @pl.when(jnp.logical_and(jnp.logical_not(first), jnp.logical_and(jnp.logical_not(last), even)))
    def _():
        project(0)
        combine(1)

    @pl.when(jnp.logical_and(jnp.logical_not(last), jnp.logical_not(even)))
    def _():
        project(1)
        combine(0)

    @pl.when(last)
    def _():
        combine((n_blocks - 1) % 2)
        o_ref[...] = x1_ref[...] + acc_ref[...].T


def _peer_expert(xn, fa, fn, fb, fr, u, vt, x1):
    n = xn.shape[0]
    t = TOK_TILE
    eb = EXP_ROWS * N_KEYS
    n_blocks = u.shape[0] // eb
    rt = t // ROUTE_TILE
    tok = lambda i, s: (i, 0)
    proj = lambda s: jnp.minimum(s, n_blocks - 1)
    comb = lambda s: jnp.maximum(s - 1, 0)
    return pl.pallas_call(
        functools.partial(_peer_expert_body, n_blocks=n_blocks),
        grid=(n // t, n_blocks + 1),
        in_specs=[
            pl.BlockSpec((t, D_MODEL), tok),
            pl.BlockSpec((rt, PEER_HEADS, EXP_ROWS, ROUTE_TILE), lambda i, s: (i, 0, comb(s), 0)),
            pl.BlockSpec((rt, PEER_HEADS, EXP_ROWS, ROUTE_TILE), lambda i, s: (i, 0, comb(s), 0)),
            pl.BlockSpec((rt, PEER_HEADS, N_KEYS, ROUTE_TILE), lambda i, s: (i, 0, 0, 0)),
            pl.BlockSpec((rt, PEER_HEADS, N_KEYS, ROUTE_TILE), lambda i, s: (i, 0, 0, 0)),
            pl.BlockSpec((eb, D_MODEL), lambda i, s: (proj(s), 0)),
            pl.BlockSpec((None, D_MODEL, eb), lambda i, s: (comb(s), 0, 0)),
            pl.BlockSpec((t, D_MODEL), tok),
        ],
        out_specs=pl.BlockSpec((t, D_MODEL), tok),
        out_shape=jax.ShapeDtypeStruct((n, D_MODEL), F32),
        scratch_shapes=[pltpu.VMEM((D_MODEL, t), F32), pltpu.VMEM((eb, t), F32),
                        pltpu.VMEM((eb, t), F32)],
        compiler_params=_cparams(("arbitrary", "arbitrary")),
        name="peer_expert",
    )(xn, fa, fn, fb, fr, u, vt, x1)


def _selectors():
    wide = D_ATTN + 4 * LANES
    selq = np.zeros((N_HEADS, wide, AUG), np.float32)
    selk = np.zeros((N_HEADS, wide, AUG), np.float32)
    selv = np.zeros((N_HEADS, AUG, D_ATTN + LANES), np.float32)
    ones_row = D_ATTN + 3 * LANES
    for h in range(N_HEADS):
        for d in range(HEAD_DIM):
            selq[h, h * HEAD_DIM + d, d] = ATTN_SCALE
            selk[h, h * HEAD_DIM + d, d] = 1.0
            selv[h, d, h * HEAD_DIM + d] = 1.0
        for piece in range(3):
            selq[h, D_ATTN + piece * LANES + h, HEAD_DIM + piece] = 1.0
            selq[h, ones_row, HEAD_DIM + 3 + piece] = 1.0
            selk[h, ones_row, HEAD_DIM + piece] = 1.0
            selk[h, D_ATTN + piece * LANES + h, HEAD_DIM + 3 + piece] = -1.0
        selv[h, HEAD_DIM, D_ATTN] = 1.0
    return (jnp.asarray(selq, BF16), jnp.asarray(selk, BF16), jnp.asarray(selv, BF16))


def kernel(x_prompt, x_sample, cache_k, cache_v, cache_logf, state_pool, page_table,
           norm_mix, w_in, b_forget, q_norm, k_norm, pool_w, pool_scale, w_out,
           norm_ffn, peer_wq, peer_subkeys, peer_u, peer_v):
    batch, seq, _ = x_prompt.shape
    n_seq, dec_seq, _ = x_sample.shape
    depth = w_in.shape[0]
    n_pool = cache_k.shape[1]
    n_pages = page_table.shape[1]
    past = n_pages * PAGE_SIZE
    assert dec_seq == 1 and seq % KV_TILE == 0 and n_pages % PAGES_PER_STEP == 0
    assert SEQ_TILE == KV_TILE and TOK_TILE % ROUTE_TILE == 0
    assert n_seq % 8 == 0
    n_prompt = batch * seq
    n_tok = n_prompt + n_seq
    n_all = -(-n_tok // TOK_TILE) * TOK_TILE

    x = jnp.concatenate([x_prompt.reshape(n_prompt, D_MODEL), x_sample.reshape(n_seq, D_MODEL),
                         jnp.zeros((n_all - n_tok, D_MODEL), F32)], axis=0)
    selq, selk, selv = _selectors()
    hm = jnp.asarray(np.kron(np.eye(N_HEADS), np.full((HEAD_DIM, HEAD_DIM), 1.0 / HEAD_DIM)), BF16)
    jj = np.arange(PAGE_SIZE)
    sfx = jnp.asarray(np.concatenate([(jj[:, None] > jj[None, :]).astype(np.float32),
                                      np.ones((PAGE_SIZE, PAGE_SIZE), np.float32)], axis=1), BF16)
    cache_kt = jnp.transpose(cache_k, (0, 1, 3, 4, 2))
    cache_vt = jnp.transpose(cache_v, (0, 1, 3, 4, 2))
    cache_lft = jnp.transpose(cache_logf, (0, 1, 3, 2))

    kp_l, vp_l, fp_l, pp_l, ks_l, vs_l, fs_l, ps_l = [], [], [], [], [], [], [], []
    for l in range(depth):
        wm = w_in[l][:, :D_POOL + 3 * D_ATTN].astype(BF16)
        wf = jnp.pad(w_in[l][:, D_POOL + 3 * D_ATTN:], ((0, 0), (0, LANES - N_HEADS))).astype(BF16)
        bfp = jnp.pad(b_forget[l], (0, LANES - N_HEADS)).reshape(1, LANES)
        xp, q, k, v, lf = _inproj(x, norm_mix[l].reshape(1, D_MODEL), wm, wf, bfp,
                                  jnp.tile(q_norm[l], N_HEADS).reshape(1, D_ATTN),
                                  jnp.tile(k_norm[l], N_HEADS).reshape(1, D_ATTN), hm)
        pw = pool_w[l].astype(BF16)
        ps = pool_scale[l].reshape(1, D_POOL)

        po_p, qa, ka, vt = _poolprep(xp, lf, q, k, v, pw, ps, selq, selk, selv, batch, seq)
        at_p = _flash(qa, ka, vt, batch, seq)

        sl = slice(n_prompt, n_tok)
        xp_s = xp[sl]
        po_s = _sample_pool(xp_s, jnp.transpose(state_pool[l], (1, 0, 2)), pw, ps, past)
        row = lambda t: t[sl].reshape(n_seq, 1, t.shape[1])
        at_s = _sample_attn(page_table, row(q), row(k), row(v), row(lf), sfx,
                            cache_kt, cache_vt, cache_lft, l).reshape(n_seq, D_ATTN)

        hp = N_HEADS // 2
        po = jnp.concatenate([po_p, po_s, jnp.zeros((n_all - n_tok, D_POOL), F32)], axis=0)
        at_s = jnp.transpose(at_s.reshape(n_seq, hp, 2 * HEAD_DIM), (1, 0, 2))
        at = jnp.concatenate([at_p, at_s, jnp.zeros((hp, n_all - n_tok, 2 * HEAD_DIM), F32)], axis=1)
        x1, xn = _outproj(x, po, at, w_out[l].astype(BF16), norm_ffn[l].reshape(1, D_MODEL))
        fa, fn, fb, fr = _peer_route(xn, peer_wq[l].astype(BF16), peer_subkeys[l].astype(BF16))
        eb = EXP_ROWS * N_KEYS
        vt_blocks = jnp.transpose(peer_v[l].reshape(-1, eb, D_MODEL), (0, 2, 1)).astype(BF16)
        x = _peer_expert(xn, fa, fn, fb, fr, peer_u[l].astype(BF16), vt_blocks, x1)

        kp_l.append(k[:n_prompt].reshape(batch, seq, N_HEADS, HEAD_DIM))
        vp_l.append(v[:n_prompt].reshape(batch, seq, N_HEADS, HEAD_DIM))
        fp_l.append(lf[:n_prompt, :N_HEADS].reshape(batch, seq, N_HEADS))
        pp_l.append(xp[:n_prompt].reshape(batch, seq, D_POOL)[:, seq - POOL_STATE:])
        ks_l.append(k[sl].reshape(n_seq, 1, N_HEADS, HEAD_DIM))
        vs_l.append(v[sl].reshape(n_seq, 1, N_HEADS, HEAD_DIM))
        fs_l.append(lf[sl, :N_HEADS].reshape(n_seq, 1, N_HEADS))
        ps_l.append(jnp.concatenate([state_pool[l][:, 1:], xp_s[:, None, :]], axis=1))

    return (x[:n_prompt].reshape(batch, seq, D_MODEL), x[sl].reshape(n_seq, 1, D_MODEL),
            jnp.stack(kp_l), jnp.stack(vp_l), jnp.stack(fp_l), jnp.stack(pp_l),
            jnp.stack(ks_l), jnp.stack(vs_l), jnp.stack(fs_l), jnp.stack(ps_l))
```

```python
import functools

import jax
import jax.numpy as jnp
import numpy as np
from jax import lax
from jax.experimental import pallas as pl
from jax.experimental.pallas import tpu as pltpu

F32 = jnp.float32
BF16 = jnp.bfloat16

EPS = 1e-6
D_MODEL = 1024
D_POOL = 512
POOL_WINDOWS = (2, 4, 8, 16)
POOL_GROUP = 128
POOL_STATE = 15
N_HEADS = 8
HEAD_DIM = 64
D_ATTN = N_HEADS * HEAD_DIM
PAGE_SIZE = 128
PEER_HEADS = 8
PEER_TOPK = 16
N_KEYS = 128
D_HALF = 128
ATTN_SCALE = HEAD_DIM ** -0.5

LANES = 128
AUG = 128
V_ROWS = 80
TOK_TILE = 512
SEQ_TILE = 512
ROUTE_TILE = 256
Q_TILE = 256
KV_TILE = 512
EXP_ROWS = 8
PAGES_PER_STEP = 16
VMEM_LIMIT = 56 * 1024 * 1024
NEG = -1e30


def _cparams(sem):
    return pltpu.CompilerParams(dimension_semantics=sem, vmem_limit_bytes=VMEM_LIMIT)


def _split3(x):
    hi = x.astype(BF16)
    r = x - hi.astype(F32)
    mid = r.astype(BF16)
    lo = (r - mid.astype(F32)).astype(BF16)
    return hi, mid, lo


def _dot(a, b):
    return jnp.dot(a, b, preferred_element_type=F32)


def _dot_nt(a, b):
    return lax.dot_general(a, b, (((1,), (1,)), ((), ())), preferred_element_type=F32)


def _inproj_body(x_ref, g_ref, wm_ref, wf_ref, bf_ref, qg_ref, kg_ref, hm_ref,
                 xp_ref, q_ref, k_ref, v_ref, lf_ref):
    x = x_ref[...]
    ms = jnp.mean(x * x, axis=-1, keepdims=True)
    h = (x * lax.rsqrt(ms + EPS) * g_ref[...]).astype(BF16)
    z = _dot(h, wm_ref[...])
    xp_ref[...] = z[:, 0:D_POOL]
    hm = hm_ref[...]

    def headnorm(t, gain):
        sq = t * t
        hi = sq.astype(BF16)
        lo = (sq - hi.astype(F32)).astype(BF16)
        msh = _dot(hi, hm) + _dot(lo, hm)
        return t * lax.rsqrt(msh + EPS) * gain

    q_ref[...] = headnorm(z[:, D_POOL:D_POOL + D_ATTN], qg_ref[...])
    k_ref[...] = headnorm(z[:, D_POOL + D_ATTN:D_POOL + 2 * D_ATTN], kg_ref[...])
    v_ref[...] = z[:, D_POOL + 2 * D_ATTN:D_POOL + 3 * D_ATTN]
    f = _dot(h, wf_ref[...]) + bf_ref[...]
    lf_ref[...] = jnp.minimum(f, 0.0) - jnp.log1p(jnp.exp(-jnp.abs(f)))


def _inproj(x, g, wm, wf, bfp, qg, kg, hm):
    n = x.shape[0]
    t = TOK_TILE
    row = lambda i: (i, 0)
    fix = lambda i: (0, 0)
    return pl.pallas_call(
        _inproj_body,
        grid=(n // t,),
        in_specs=[
            pl.BlockSpec((t, D_MODEL), row),
            pl.BlockSpec((1, D_MODEL), fix),
            pl.BlockSpec(wm.shape, fix),
            pl.BlockSpec(wf.shape, fix),
            pl.BlockSpec((1, LANES), fix),
            pl.BlockSpec((1, D_ATTN), fix),
            pl.BlockSpec((1, D_ATTN), fix),
            pl.BlockSpec(hm.shape, fix),
        ],
        out_specs=[
            pl.BlockSpec((t, D_POOL), row),
            pl.BlockSpec((t, D_ATTN), row),
            pl.BlockSpec((t, D_ATTN), row),
            pl.BlockSpec((t, D_ATTN), row),
            pl.BlockSpec((t, LANES), row),
        ],
        out_shape=[
            jax.ShapeDtypeStruct((n, D_POOL), F32),
            jax.ShapeDtypeStruct((n, D_ATTN), F32),
            jax.ShapeDtypeStruct((n, D_ATTN), F32),
            jax.ShapeDtypeStruct((n, D_ATTN), F32),
            jax.ShapeDtypeStruct((n, LANES), F32),
        ],
        compiler_params=_cparams(("arbitrary",)),
        name="inproj",
    )(x, g, wm, wf, bfp, qg, kg, hm)


def _pool_mix(xp, window_sum, cnt_fn, pw_ref, ps_ref):
    outs = []
    for g, w in enumerate(POOL_WINDOWS):
        lanes = slice(g * POOL_GROUP, (g + 1) * POOL_GROUP)
        pooled = window_sum(g, w) / cnt_fn(w) - xp[:, lanes]
        outs.append(_dot(pooled.astype(BF16), pw_ref[g]))
    return jnp.concatenate(outs, axis=1) * ps_ref[...]


def _poolprep_body(xp_ref, lf_ref, q_ref, k_ref, v_ref, pw_ref, ps_ref,
                   selq_ref, selk_ref, selv_ref,
                   po_ref, qa_ref, ka_ref, vt_ref, xx_ref, fc_ref):
    t = SEQ_TILE
    hist = 16
    step = pl.program_id(1)

    @pl.when(step == 0)
    def _():
        xx_ref[0:hist, :] = jnp.zeros((hist, D_POOL), F32)
        fc_ref[...] = jnp.zeros_like(fc_ref)

    xp = xp_ref[...]
    xx_ref[hist:hist + t, :] = xp
    pos = lax.broadcasted_iota(jnp.int32, (t, POOL_GROUP), 0) + step * t

    def window_sum(g, w):
        lanes = slice(g * POOL_GROUP, (g + 1) * POOL_GROUP)
        ws = xp[:, lanes]
        for r in range(1, w):
            ws = ws + xx_ref[hist - r:hist - r + t, lanes]
        return ws

    def cnt(w):
        return jnp.minimum(pos + 1, w).astype(F32)

    po_ref[...] = _pool_mix(xp, window_sum, cnt, pw_ref, ps_ref)
    xx_ref[0:hist, :] = xx_ref[t:t + hist, :]

    ri = lax.broadcasted_iota(jnp.int32, (t, t), 0)
    ci = lax.broadcasted_iota(jnp.int32, (t, t), 1)
    tri = jnp.where(ci <= ri, 1.0, 0.0).astype(BF16)
    hi, mid, lo = _split3(lf_ref[...])
    fcum = _dot(tri, hi) + _dot(tri, mid) + _dot(tri, lo) + fc_ref[...]
    fc_ref[...] = fcum[t - 1:t, :]
    fh, fm, fl = _split3(fcum)
    ones = jnp.ones((t, LANES), BF16)
    wq = jnp.concatenate([q_ref[...].astype(BF16), fh, fm, fl, ones], axis=1)
    wk = jnp.concatenate([k_ref[...].astype(BF16), fh, fm, fl, ones], axis=1)
    wv = jnp.concatenate([v_ref[...].astype(BF16), ones], axis=1)
    for h in range(N_HEADS):
        qa_ref[h] = _dot(wq, selq_ref[h]).astype(BF16)
        ka_ref[h] = _dot(wk, selk_ref[h]).astype(BF16)
        vt_ref[h] = _dot_nt(selv_ref[h], wv).astype(BF16)


def _poolprep(xp, lf, q, k, v, pw, ps, selq, selk, selv, batch, seq):
    t = SEQ_TILE
    nt = seq // t
    n = batch * seq
    row = lambda b, i: (b * nt + i, 0)
    fix2 = lambda b, i: (0, 0)
    fix3 = lambda b, i: (0, 0, 0)
    return pl.pallas_call(
        _poolprep_body,
        grid=(batch, nt),
        in_specs=[
            pl.BlockSpec((t, D_POOL), row),
            pl.BlockSpec((t, LANES), row),
            pl.BlockSpec((t, D_ATTN), row),
            pl.BlockSpec((t, D_ATTN), row),
            pl.BlockSpec((t, D_ATTN), row),
            pl.BlockSpec(pw.shape, fix3),
            pl.BlockSpec((1, D_POOL), fix2),
            pl.BlockSpec(selq.shape, fix3),
            pl.BlockSpec(selk.shape, fix3),
            pl.BlockSpec(selv.shape, fix3),
        ],
        out_specs=[
            pl.BlockSpec((t, D_POOL), row),
            pl.BlockSpec((N_HEADS, t, AUG), lambda b, i: (0, b * nt + i, 0)),
            pl.BlockSpec((N_HEADS, t, AUG), lambda b, i: (0, b * nt + i, 0)),
            pl.BlockSpec((N_HEADS, None, AUG, t), lambda b, i: (0, b * nt + i, 0, 0)),
        ],
        out_shape=[
            jax.ShapeDtypeStruct((n, D_POOL), F32),
            jax.ShapeDtypeStruct((N_HEADS, n, AUG), BF16),
            jax.ShapeDtypeStruct((N_HEADS, n, AUG), BF16),
            jax.ShapeDtypeStruct((N_HEADS, n // t, AUG, t), BF16),
        ],
        scratch_shapes=[
            pltpu.VMEM((t + 16, D_POOL), F32),
            pltpu.VMEM((1, LANES), F32),
        ],
        compiler_params=_cparams(("arbitrary", "arbitrary")),
        name="poolprep",
    )(xp, lf, q, k, v, pw, ps, selq, selk, selv)


def _flash_body(qa_ref, ka_ref, vt_ref, o_ref, s0_scr, s1_scr):
    tq, tk = Q_TILE, KV_TILE
    i = pl.program_id(2)
    n_full = (i * tq) // tk
    qpos = lax.broadcasted_iota(jnp.int32, (tk, tq), 1) + i * tq
    krel = lax.broadcasted_iota(jnp.int32, (tk, tq), 0)
    vrows = V_ROWS

    s_slots = (s0_scr, s1_scr)

    def scores(j, slot):
        start = pl.multiple_of(j * tk, tk)
        for hh in range(2):
            s_slots[slot][hh] = _dot_nt(ka_ref[hh, pl.ds(start, tk), :], qa_ref[hh])

    def tile(j, slot, carry, masked):
        if not masked:
            scores(j + 1, 1 - slot)
        new = []
        for hh in range(2):
            m, acc = carry[hh]
            s = s_slots[slot][hh]
            if masked:
                s = jnp.where(krel + j * tk <= qpos, s, NEG)
            m_new = jnp.maximum(m, jnp.max(s, axis=0, keepdims=True))
            p = jnp.exp(s - m_new).astype(BF16)
            alpha = jnp.exp(m - m_new)
            acc = alpha * acc + _dot(vt_ref[hh, j, 0:vrows, :], p)
            new.append((m_new, acc))
        return tuple(new)

    def pair(jj, carry):
        carry = tile(2 * jj, 0, carry, False)
        return tile(2 * jj + 1, 1, carry, False)

    scores(0, 0)
    one = (jnp.full((1, tq), NEG, F32), jnp.zeros((vrows, tq), F32))
    carry = lax.fori_loop(0, n_full // 2, pair, (one, one))
    j0 = 2 * (n_full // 2)
    carry = lax.cond(
        n_full % 2 == 1,
        lambda c: tile(j0 + 1, 1, tile(j0, 0, c, False), True),
        lambda c: tile(j0, 0, c, True),
        carry)
    outs = []
    for hh in range(2):
        acc = jnp.concatenate([carry[hh][1], jnp.zeros((AUG - vrows, tq), F32)], axis=0)
        acc_t = acc.T
        outs.append(acc_t[:, 0:HEAD_DIM] / acc_t[:, HEAD_DIM:HEAD_DIM + 1])
    o_ref[...] = jnp.concatenate(outs, axis=1)


def _flash(qa, ka, vt, batch, seq):
    tq = Q_TILE
    nq = seq // tq
    n = batch * seq
    return pl.pallas_call(
        _flash_body,
        grid=(batch, N_HEADS // 2, nq),
        in_specs=[
            pl.BlockSpec((2, tq, AUG), lambda b, hp, i: (hp, b * nq + i, 0)),
            pl.BlockSpec((2, seq, AUG), lambda b, hp, i: (hp, b, 0)),
            pl.BlockSpec((2, seq // KV_TILE, AUG, KV_TILE), lambda b, hp, i: (hp, b, 0, 0)),
        ],
        out_specs=pl.BlockSpec((None, tq, 2 * HEAD_DIM), lambda b, hp, i: (hp, b * nq + i, 0)),
        out_shape=jax.ShapeDtypeStruct((N_HEADS // 2, n, 2 * HEAD_DIM), F32),
        scratch_shapes=[pltpu.VMEM((2, KV_TILE, Q_TILE), F32), pltpu.VMEM((2, KV_TILE, Q_TILE), F32)],
        compiler_params=_cparams(("arbitrary", "arbitrary", "arbitrary")),
        name="flash",
    )(qa, ka, vt)


def _sample_pool_body(xp_ref, st_ref, pw_ref, ps_ref, o_ref, *, start):
    xp = xp_ref[...]

    def window_sum(g, w):
        lanes = slice(g * POOL_GROUP, (g + 1) * POOL_GROUP)
        ws = xp[:, lanes]
        for r in range(1, w):
            ws = ws + st_ref[POOL_STATE - r][:, lanes]
        return ws

    o_ref[...] = _pool_mix(xp, window_sum, lambda w: float(min(start + 1, w)), pw_ref, ps_ref)


def _sample_pool(xp_s, state_t, pw, ps, start):
    return pl.pallas_call(
        functools.partial(_sample_pool_body, start=start),
        out_shape=jax.ShapeDtypeStruct(xp_s.shape, F32),
        compiler_params=pltpu.CompilerParams(vmem_limit_bytes=VMEM_LIMIT),
        name="sample_pool",
    )(xp_s, state_t, pw, ps)


def _sample_attn_body(pt_ref, q_ref, k_ref, v_ref, lf_ref, sfx_ref, *refs, n_steps):
    g_pages = PAGES_PER_STEP
    ck = refs[0:g_pages]
    cv = refs[g_pages:2 * g_pages]
    cl = refs[2 * g_pages:3 * g_pages]
    o_ref = refs[3 * g_pages]
    m_ref, l_ref, acc_ref, car_ref = refs[3 * g_pages + 1:]
    rows = 2 * N_HEADS
    step = pl.program_id(1)

    hrow = lax.broadcasted_iota(jnp.int32, (rows, D_ATTN), 0)
    hcol = lax.broadcasted_iota(jnp.int32, (rows, D_ATTN), 1) // HEAD_DIM
    headmask = hrow == hcol
    qf = q_ref[0].astype(BF16).astype(F32)
    qbd = jnp.where(headmask, qf * ATTN_SCALE, 0.0)

    @pl.when(step == 0)
    def _():
        m_ref[...] = jnp.full(m_ref.shape, NEG, F32)
        l_ref[...] = jnp.zeros_like(l_ref)
        acc_ref[...] = jnp.zeros_like(acc_ref)
        eye = (lax.broadcasted_iota(jnp.int32, (rows, LANES), 0)
               == lax.broadcasted_iota(jnp.int32, (rows, LANES), 1))
        c_new = jnp.sum(jnp.where(eye, lf_ref[0], 0.0), axis=1, keepdims=True)
        car_ref[...] = jnp.broadcast_to(c_new, car_ref.shape)

    carry = car_ref[...]
    sfx = sfx_ref[...]
    bias = []
    for g in range(g_pages):
        lft = cl[g][...]
        hi, mid, lo = _split3(jnp.concatenate([lft, jnp.zeros_like(lft)], axis=0))
        r = _dot(hi, sfx) + _dot(mid, sfx) + _dot(lo, sfx)
        bias.append(r[:, 0:PAGE_SIZE] + carry)
        carry = carry + r[:, PAGE_SIZE:2 * PAGE_SIZE]
    car_ref[...] = carry
    kt_all = jnp.concatenate(
        [ck[g][...].reshape(D_ATTN, PAGE_SIZE).astype(BF16) for g in range(g_pages)], axis=1)
    vt_all = jnp.concatenate(
        [cv[g][...].reshape(D_ATTN, PAGE_SIZE).astype(BF16) for g in range(g_pages)], axis=1)
    s_all = _dot(qbd.astype(BF16), kt_all) + jnp.concatenate(bias, axis=1)
    m_old = m_ref[...]
    m_new = jnp.maximum(m_old, jnp.max(s_all, axis=1, keepdims=True))
    p = jnp.exp(s_all - m_new)
    alpha = jnp.exp(m_old - m_new)
    l_ref[...] = alpha * l_ref[...] + jnp.sum(p, axis=1, keepdims=True)
    acc_ref[...] = alpha * acc_ref[...] + _dot_nt(p.astype(BF16), vt_all)
    m_ref[...] = m_new

    @pl.when(step == n_steps - 1)
    def _():
        kf = k_ref[0].astype(BF16).astype(F32)
        vf = v_ref[0].astype(BF16).astype(F32)
        s_new = jnp.sum(qbd * kf, axis=1, keepdims=True)
        m_old = m_ref[...]
        m_fin = jnp.maximum(m_old, s_new)
        a = jnp.exp(m_old - m_fin)
        p_new = jnp.exp(s_new - m_fin)
        l_fin = a * l_ref[...] + p_new
        acc = a * acc_ref[...] + p_new.astype(BF16).astype(F32) * vf
        o_ref[0] = jnp.sum(jnp.where(headmask, acc / l_fin, 0.0), axis=0, keepdims=True)


def _sample_attn(page_table, q_s, k_s, v_s, lf_s, sfx, cache_kt, cache_vt, cache_lft, layer):
    n_seq, n_pages = page_table.shape
    g_pages = PAGES_PER_STEP
    n_steps = n_pages // g_pages
    rows = 2 * N_HEADS

    def page_map(g, nd):
        def index(b, s, pt):
            return (layer, pt[b, n_pages - 1 - (s * g_pages + g)]) + (0,) * nd
        return index

    tok = lambda b, s, pt: (b, 0, 0)
    in_specs = [
        pl.BlockSpec((1, 1, D_ATTN), tok),
        pl.BlockSpec((1, 1, D_ATTN), tok),
        pl.BlockSpec((1, 1, D_ATTN), tok),
        pl.BlockSpec((1, 1, LANES), tok),
        pl.BlockSpec(sfx.shape, lambda b, s, pt: (0, 0)),
    ]
    page = (None, None, N_HEADS, HEAD_DIM, PAGE_SIZE)
    in_specs += [pl.BlockSpec(page, page_map(g, 3)) for g in range(g_pages)]
    in_specs += [pl.BlockSpec(page, page_map(g, 3)) for g in range(g_pages)]
    in_specs += [pl.BlockSpec((None, None, N_HEADS, PAGE_SIZE), page_map(g, 2)) for g in range(g_pages)]
    grid_spec = pltpu.PrefetchScalarGridSpec(
        num_scalar_prefetch=1,
        grid=(n_seq, n_steps),
        in_specs=in_specs,
        out_specs=pl.BlockSpec((1, 1, D_ATTN), tok),
        scratch_shapes=[
            pltpu.VMEM((rows, 1), F32),
            pltpu.VMEM((rows, 1), F32),
            pltpu.VMEM((rows, D_ATTN), F32),
            pltpu.VMEM((rows, LANES), F32),
        ],
    )
    return pl.pallas_call(
        functools.partial(_sample_attn_body, n_steps=n_steps),
        grid_spec=grid_spec,
        out_shape=jax.ShapeDtypeStruct((n_seq, 1, D_ATTN), F32),
        compiler_params=_cparams(("arbitrary", "arbitrary")),
        name="sample_attn",
    )(page_table, q_s, k_s, v_s, lf_s, sfx,
      *([cache_kt] * g_pages), *([cache_vt] * g_pages), *([cache_lft] * g_pages))


def _outproj_body(x_ref, po_ref, at_ref, wo_ref, g_ref, x1_ref, xn_ref):
    mix = jnp.concatenate([po_ref[...]] + [at_ref[hp] for hp in range(N_HEADS // 2)],
                          axis=1).astype(BF16)
    x1 = x_ref[...] + _dot(mix, wo_ref[...])
    x1_ref[...] = x1
    ms = jnp.mean(x1 * x1, axis=-1, keepdims=True)
    xn_ref[...] = (x1 * lax.rsqrt(ms + EPS) * g_ref[...]).astype(BF16)


def _outproj(x, po, at, wo, g):
    n = x.shape[0]
    t = TOK_TILE
    row = lambda i: (i, 0)
    fix = lambda i: (0, 0)
    return pl.pallas_call(
        _outproj_body,
        grid=(n // t,),
        in_specs=[
            pl.BlockSpec((t, D_MODEL), row),
            pl.BlockSpec((t, D_POOL), row),
            pl.BlockSpec((N_HEADS // 2, t, 2 * HEAD_DIM), lambda i: (0, i, 0)),
            pl.BlockSpec(wo.shape, fix),
            pl.BlockSpec((1, D_MODEL), fix),
        ],
        out_specs=[pl.BlockSpec((t, D_MODEL), row), pl.BlockSpec((t, D_MODEL), row)],
        out_shape=[jax.ShapeDtypeStruct((n, D_MODEL), F32), jax.ShapeDtypeStruct((n, D_MODEL), BF16)],
        compiler_params=_cparams(("arbitrary",)),
        name="outproj",
    )(x, po, at, wo, g)


def _top16_ranks(s, sv_ref, half, rowi):
    rank = jnp.full(s.shape, float(PEER_TOPK), F32)
    for c in range(PEER_TOPK):
        m = jnp.max(s, axis=0, keepdims=True)
        first = jnp.min(jnp.where(s == m, rowi, float(N_KEYS)), axis=0, keepdims=True)
        sel = rowi == first
        rank = jnp.where(sel, float(c), rank)
        s = jnp.where(sel, -jnp.inf, s)
        sv_ref[half, c:c + 1, :] = m
    return rank


def _top16_ranks_distinct(halves, sv_ref):
    ss = list(halves)
    ranks = [jnp.full(s.shape, float(PEER_TOPK), F32) for s in ss]
    for c in range(PEER_TOPK):
        for half in range(len(ss)):
            m = jnp.max(ss[half], axis=0, keepdims=True)
            sel = ss[half] == m
            ranks[half] = jnp.where(sel, float(c), ranks[half])
            ss[half] = jnp.where(sel, -jnp.inf, ss[half])
            sv_ref[half, c:c + 1, :] = m
    return ranks


def _pair_segments():
    segs, row = [], 0
    for c in range(PEER_TOPK):
        nd = PEER_TOPK // (c + 1)
        if nd < 8:
            break
        segs.append((row, c, nd))
        row += nd
    bins = []
    for c in range(len(segs), PEER_TOPK):
        nd = PEER_TOPK // (c + 1)
        for bn in bins:
            if bn[0] + nd <= 8:
                bn[1].append((bn[0], c, nd))
                bn[0] += nd
                break
        else:
            bins.append([nd, [(0, c, nd)]])
    for bn in bins:
        segs += [(row + o, c, nd) for o, c, nd in bn[1]]
        row += 8
    return segs, row


def _peer_route_body(xn_ref, wq_ref, sk_ref, flat_ref, a_ref, n_ref, b_ref, r_ref,
                     q_scr, sv_ref, rk_ref, cand_scr, wgt_scr):
    t = xn_ref.shape[0]
    q = _dot(xn_ref[...], wq_ref[...]).astype(BF16)
    for hp in range(2 * PEER_HEADS):
        q_scr[hp] = q[:, hp * D_HALF:(hp + 1) * D_HALF]
    rowi = lax.broadcasted_iota(jnp.int32, (N_KEYS, t), 0).astype(F32)
    segs, n_rows = _pair_segments()
    flat = flat_ref[...]
    big = float(PEER_TOPK * PEER_TOPK)
    cand_scr[...] = jnp.full(cand_scr.shape, -jnp.inf, F32)
    wgt_scr[...] = jnp.zeros_like(wgt_scr)

    def head(h, _):
        s1 = _dot_nt(sk_ref[0], q_scr[2 * h])
        s2 = _dot_nt(sk_ref[1], q_scr[2 * h + 1])
        rk_ref[0], rk_ref[1] = _top16_ranks_distinct((s1, s2), sv_ref)
        ranked = (jnp.sum(jnp.where(rk_ref[0] < float(PEER_TOPK), 1.0, 0.0), axis=0, keepdims=True)
                  + jnp.sum(jnp.where(rk_ref[1] < float(PEER_TOPK), 1.0, 0.0), axis=0, keepdims=True))

        @pl.when(jnp.max(jnp.abs(ranked - float(2 * PEER_TOPK))) > 0.0)
        def _():
            rk_ref[0] = _top16_ranks(s1, sv_ref, 0, rowi)
            rk_ref[1] = _top16_ranks(s2, sv_ref, 1, rowi)

        r1 = rk_ref[0]
        r2 = rk_ref[1]
        sv1 = sv_ref[0]
        sv2 = sv_ref[1]
        e1 = jnp.exp(sv1 - sv1[0:1, :])
        e2 = jnp.exp(sv2 - sv2[0:1, :])
        for row, c, nd in segs:
            cand_scr[row:row + nd, :] = sv1[c:c + 1, :] + sv2[0:nd, :]
            wgt_scr[row:row + nd, :] = e1[c:c + 1, :] * e2[0:nd, :]
        cand = cand_scr[...]
        chosen = jnp.zeros(cand.shape, F32)
        for _k in range(PEER_TOPK):
            m = jnp.max(cand, axis=0, keepdims=True)
            first = jnp.min(jnp.where(cand == m, flat, big), axis=0, keepdims=True)
            sel = flat == first
            chosen = jnp.where(sel, 1.0, chosen)
            cand = jnp.where(sel, -jnp.inf, cand)
        z = jnp.sum(chosen * wgt_scr[...], axis=0, keepdims=True)
        nrow = jnp.zeros((N_KEYS, t), F32)
        for row, c, nd in segs:
            n_c = jnp.sum(chosen[row:row + nd, :], axis=0, keepdims=True)
            nrow = jnp.where(r1 == float(c), n_c, nrow)
        a_ref[h] = jnp.where(r1 < float(PEER_TOPK), jnp.exp(s1 - sv1[0:1, :]) / z, 0.0)
        n_ref[h] = nrow
        b_ref[h] = jnp.exp(s2 - sv2[0:1, :]).astype(BF16)
        r_ref[h] = r2.astype(BF16)
        return 0

    lax.fori_loop(0, PEER_HEADS, head, 0)


def _peer_route(xn, wq, sk):
    n = xn.shape[0]
    t = ROUTE_TILE
    fac = pl.BlockSpec((None, PEER_HEADS, N_KEYS, t), lambda i: (i, 0, 0, 0))
    shp = jax.ShapeDtypeStruct((n // t, PEER_HEADS, N_KEYS, t), F32)
    shp_bf = jax.ShapeDtypeStruct((n // t, PEER_HEADS, N_KEYS, t), BF16)
    segs, n_rows = _pair_segments()
    flat = np.full((n_rows, t), 1e9, np.float32)
    for row, c, nd in segs:
        flat[row:row + nd, :] = (c * PEER_TOPK + np.arange(nd, dtype=np.float32))[:, None]
    return pl.pallas_call(
        _peer_route_body,
        grid=(n // t,),
        in_specs=[
            pl.BlockSpec((t, D_MODEL), lambda i: (i, 0)),
            pl.BlockSpec(wq.shape, lambda i: (0, 0)),
            pl.BlockSpec(sk.shape, lambda i: (0, 0, 0)),
            pl.BlockSpec(flat.shape, lambda i: (0, 0)),
        ],
        out_specs=[fac, fac, fac, fac],
        out_shape=[shp, shp, shp_bf, shp_bf],
        scratch_shapes=[
            pltpu.VMEM((2 * PEER_HEADS, t, D_HALF), BF16),
            pltpu.VMEM((2, PEER_TOPK, t), F32),
            pltpu.VMEM((2, N_KEYS, t), F32),
            pltpu.VMEM((n_rows, t), F32),
            pltpu.VMEM((n_rows, t), F32),
        ],
        compiler_params=_cparams(("arbitrary",)),
        name="peer_route",
    )(xn, wq, sk, jnp.asarray(flat))


def _peer_expert_body(xn_ref, a_ref, n_ref, b_ref, r_ref, u_ref, vt_ref, x1_ref,
                      o_ref, acc_ref, act0_ref, act1_ref, *, n_blocks):
    s = pl.program_id(1)
    acts = (act0_ref, act1_ref)

    def project(slot):
        acts[slot][...] = _dot_nt(u_ref[...], xn_ref[...])

    def combine(slot):
        a_rows = a_ref[...].astype(BF16)
        n_rows = n_ref[...].astype(BF16)
        zero = jnp.zeros((), BF16)
        rtile = a_ref.shape[-1]
        parts = []
        for i in range(EXP_ROWS):
            row = []
            for k in range(a_ref.shape[0]):
                act = acts[slot][i * N_KEYS:(i + 1) * N_KEYS, k * rtile:(k + 1) * rtile]
                ge = jax.nn.gelu(act.astype(BF16), approximate=True)
                gate = None
                for h in range(PEER_HEADS):
                    keep = r_ref[k, h] < n_rows[k, h, i:i + 1, :]
                    term = jnp.where(keep, b_ref[k, h], zero) * a_rows[k, h, i:i + 1, :]
                    gate = term if gate is None else gate + term
                row.append(gate * ge)
            parts.append(jnp.concatenate(row, axis=1))
        w = jnp.concatenate(parts, axis=0)
        acc_ref[...] += _dot(vt_ref[...], w)

    first = s == 0
    last = s == n_blocks
    even = s % 2 == 0

    @pl.when(first)
    def _():
        acc_ref[...] = jnp.zeros_like(acc_ref)
        project(0)

    @pl.when(jnp.logical_and(jnp.logical_not(first), jnp.logical_and(jnp.logical_not(last), even)))
    def _():
        project(0)
        combine(1)

    @pl.when(jnp.logical_and(jnp.logical_not(last), jnp.logical_not(even)))
    def _():
        project(1)
        combine(0)

    @pl.when(last)
    def _():
        combine((n_blocks - 1) % 2)
        o_ref[...] = x1_ref[...] + acc_ref[...].T


def _peer_expert(xn, fa, fn, fb, fr, u, vt, x1):
    n = xn.shape[0]
    t = TOK_TILE
    eb = EXP_ROWS * N_KEYS
    n_blocks = u.shape[0] // eb
    rt = t // ROUTE_TILE
    tok = lambda i, s: (i, 0)
    proj = lambda s: jnp.minimum(s, n_blocks - 1)
    comb = lambda s: jnp.maximum(s - 1, 0)
    return pl.pallas_call(
        functools.partial(_peer_expert_body, n_blocks=n_blocks),
        grid=(n // t, n_blocks + 1),
        in_specs=[
            pl.BlockSpec((t, D_MODEL), tok),
            pl.BlockSpec((rt, PEER_HEADS, EXP_ROWS, ROUTE_TILE), lambda i, s: (i, 0, comb(s), 0)),
            pl.BlockSpec((rt, PEER_HEADS, EXP_ROWS, ROUTE_TILE), lambda i, s: (i, 0, comb(s), 0)),
            pl.BlockSpec((rt, PEER_HEADS, N_KEYS, ROUTE_TILE), lambda i, s: (i, 0, 0, 0)),
            pl.BlockSpec((rt, PEER_HEADS, N_KEYS, ROUTE_TILE), lambda i, s: (i, 0, 0, 0)),
            pl.BlockSpec((eb, D_MODEL), lambda i, s: (proj(s), 0)),
            pl.BlockSpec((None, D_MODEL, eb), lambda i, s: (comb(s), 0, 0)),
            pl.BlockSpec((t, D_MODEL), tok),
        ],
        out_specs=pl.BlockSpec((t, D_MODEL), tok),
        out_shape=jax.ShapeDtypeStruct((n, D_MODEL), F32),
        scratch_shapes=[pltpu.VMEM((D_MODEL, t), F32), pltpu.VMEM((eb, t), F32),
                        pltpu.VMEM((eb, t), F32)],
        compiler_params=_cparams(("arbitrary", "arbitrary")),
        name="peer_expert",
    )(xn, fa, fn, fb, fr, u, vt, x1)


def _selectors():
    wide = D_ATTN + 4 * LANES
    selq = np.zeros((N_HEADS, wide, AUG), np.float32)
    selk = np.zeros((N_HEADS, wide, AUG), np.float32)
    selv = np.zeros((N_HEADS, AUG, D_ATTN + LANES), np.float32)
    ones_row = D_ATTN + 3 * LANES
    for h in range(N_HEADS):
        for d in range(HEAD_DIM):
            selq[h, h * HEAD_DIM + d, d] = ATTN_SCALE
            selk[h, h * HEAD_DIM + d, d] = 1.0
            selv[h, d, h * HEAD_DIM + d] = 1.0
        for piece in range(3):
            selq[h, D_ATTN + piece * LANES + h, HEAD_DIM + piece] = 1.0
            selq[h, ones_row, HEAD_DIM + 3 + piece] = 1.0
            selk[h, ones_row, HEAD_DIM + piece] = 1.0
            selk[h, D_ATTN + piece * LANES + h, HEAD_DIM + 3 + piece] = -1.0
        selv[h, HEAD_DIM, D_ATTN] = 1.0
    return (jnp.asarray(selq, BF16), jnp.asarray(selk, BF16), jnp.asarray(selv, BF16))


def kernel(x_prompt, x_sample, cache_k, cache_v, cache_logf, state_pool, page_table,
           norm_mix, w_in, b_forget, q_norm, k_norm, pool_w, pool_scale, w_out,
           norm_ffn, peer_wq, peer_subkeys, peer_u, peer_v):
    batch, seq, _ = x_prompt.shape
    n_seq, dec_seq, _ = x_sample.shape
    depth = w_in.shape[0]
    n_pool = cache_k.shape[1]
    n_pages = page_table.shape[1]
    past = n_pages * PAGE_SIZE
    assert dec_seq == 1 and seq % KV_TILE == 0 and n_pages % PAGES_PER_STEP == 0
    assert SEQ_TILE == KV_TILE and TOK_TILE % ROUTE_TILE == 0
    assert n_seq % 8 == 0
    n_prompt = batch * seq
    assert n_prompt % TOK_TILE == 0
    n_samp = -(-n_seq // TOK_TILE) * TOK_TILE
    hp = N_HEADS // 2

    def pad_rows(t, axis=0):
        widths = [(0, 0)] * t.ndim
        widths[axis] = (0, n_samp - n_seq)
        return jnp.pad(t, widths)

    x_p = x_prompt.reshape(n_prompt, D_MODEL)
    x_s = pad_rows(x_sample.reshape(n_seq, D_MODEL))
    selq, selk, selv = _selectors()
    hm = jnp.asarray(np.kron(np.eye(N_HEADS), np.full((HEAD_DIM, HEAD_DIM), 1.0 / HEAD_DIM)), BF16)
    jj = np.arange(PAGE_SIZE)
    sfx = jnp.asarray(np.concatenate([(jj[:, None] > jj[None, :]).astype(np.float32),
                                      np.ones((PAGE_SIZE, PAGE_SIZE), np.float32)], axis=1), BF16)
    cache_kt = jnp.transpose(cache_k, (0, 1, 3, 4, 2))
    cache_vt = jnp.transpose(cache_v, (0, 1, 3, 4, 2))
    cache_lft = jnp.transpose(cache_logf, (0, 1, 3, 2))

    kp_l, vp_l, fp_l, pp_l, ks_l, vs_l, fs_l, ps_l = [], [], [], [], [], [], [], []
    for l in range(depth):
        wm = w_in[l][:, :D_POOL + 3 * D_ATTN].astype(BF16)
        wf = jnp.pad(w_in[l][:, D_POOL + 3 * D_ATTN:], ((0, 0), (0, LANES - N_HEADS))).astype(BF16)
        bfp = jnp.pad(b_forget[l], (0, LANES - N_HEADS)).reshape(1, LANES)
        inproj = lambda t: _inproj(t, norm_mix[l].reshape(1, D_MODEL), wm, wf, bfp,
                                   jnp.tile(q_norm[l], N_HEADS).reshape(1, D_ATTN),
                                   jnp.tile(k_norm[l], N_HEADS).reshape(1, D_ATTN), hm)
        xp, q, k, v, lf = inproj(x_p)
        xp_s, q_s, k_s, v_s, lf_s = (t[:n_seq] for t in inproj(x_s))
        pw = pool_w[l].astype(BF16)
        ps = pool_scale[l].reshape(1, D_POOL)

        po_p, qa, ka, vt = _poolprep(xp, lf, q, k, v, pw, ps, selq, selk, selv, batch, seq)
        at_p = _flash(qa, ka, vt, batch, seq)

        po_s = _sample_pool(xp_s, jnp.transpose(state_pool[l], (1, 0, 2)), pw, ps, past)
        row = lambda t: t.reshape(n_seq, 1, t.shape[1])
        at_s = _sample_attn(page_table, row(q_s), row(k_s), row(v_s), row(lf_s), sfx,
                            cache_kt, cache_vt, cache_lft, l).reshape(n_seq, hp, 2 * HEAD_DIM)
        at_s = pad_rows(jnp.transpose(at_s, (1, 0, 2)), axis=1)

        wo = w_out[l].astype(BF16)
        gf = norm_ffn[l].reshape(1, D_MODEL)
        wq = peer_wq[l].astype(BF16)
        sk = peer_subkeys[l].astype(BF16)
        u = peer_u[l].astype(BF16)
        eb = EXP_ROWS * N_KEYS
        vt_blocks = jnp.transpose(peer_v[l].reshape(-1, eb, D_MODEL), (0, 2, 1)).astype(BF16)

        def channel_mix(x_in, po, at):
            x1, xn = _outproj(x_in, po, at, wo, gf)
            fa, fn, fb, fr = _peer_route(xn, wq, sk)
            return _peer_expert(xn, fa, fn, fb, fr, u, vt_blocks, x1)

        x_p = channel_mix(x_p, po_p, at_p)
        x_s = channel_mix(x_s, pad_rows(po_s), at_s)

        kp_l.append(k.reshape(batch, seq, N_HEADS, HEAD_DIM))
        vp_l.append(v.reshape(batch, seq, N_HEADS, HEAD_DIM))
        fp_l.append(lf[:, :N_HEADS].reshape(batch, seq, N_HEADS))
        pp_l.append(xp.reshape(batch, seq, D_POOL)[:, seq - POOL_STATE:])
        ks_l.append(k_s.reshape(n_seq, 1, N_HEADS, HEAD_DIM))
        vs_l.append(v_s.reshape(n_seq, 1, N_HEADS, HEAD_DIM))
        fs_l.append(lf_s[:, :N_HEADS].reshape(n_seq, 1, N_HEADS))
        ps_l.append(jnp.concatenate([state_pool[l][:, 1:], xp_s[:, None, :]], axis=1))

    return (x_p.reshape(batch, seq, D_MODEL), x_s[:n_seq].reshape(n_seq, 1, D_MODEL),
            jnp.stack(kp_l), jnp.stack(vp_l), jnp.stack(fp_l), jnp.stack(pp_l),
            jnp.stack(ks_l), jnp.stack(vs_l), jnp.stack(fs_l), jnp.stack(ps_l))
```

```python
import functools

import jax
import jax.numpy as jnp
import numpy as np
from jax import lax
from jax.experimental import pallas as pl
from jax.experimental.pallas import tpu as pltpu

F32 = jnp.float32
BF16 = jnp.bfloat16

EPS = 1e-6
D_MODEL = 1024
D_POOL = 512
POOL_WINDOWS = (2, 4, 8, 16)
POOL_GROUP = 128
POOL_STATE = 15
N_HEADS = 8
HEAD_DIM = 64
D_ATTN = N_HEADS * HEAD_DIM
PAGE_SIZE = 128
PEER_HEADS = 8
PEER_TOPK = 16
N_KEYS = 128
D_HALF = 128
ATTN_SCALE = HEAD_DIM ** -0.5

LANES = 128
AUG = 128
V_ROWS = 80
TOK_TILE = 512
SEQ_TILE = 512
ROUTE_TILE = 256
Q_TILE = 256
KV_TILE = 512
EXP_ROWS = 8
PAGES_PER_STEP = 16
VMEM_LIMIT = 56 * 1024 * 1024
NEG = -1e30


def _cparams(sem):
    return pltpu.CompilerParams(dimension_semantics=sem, vmem_limit_bytes=VMEM_LIMIT)


def _split3(x):
    hi = x.astype(BF16)
    r = x - hi.astype(F32)
    mid = r.astype(BF16)
    lo = (r - mid.astype(F32)).astype(BF16)
    return hi, mid, lo


def _dot(a, b):
    return jnp.dot(a, b, preferred_element_type=F32)


def _dot_nt(a, b):
    return lax.dot_general(a, b, (((1,), (1,)), ((), ())), preferred_element_type=F32)


def _inproj_body(x_ref, g_ref, wm_ref, wf_ref, bf_ref, qg_ref, kg_ref, hm_ref,
                 xp_ref, q_ref, k_ref, v_ref, lf_ref):
    x = x_ref[...]
    ms = jnp.mean(x * x, axis=-1, keepdims=True)
    h = (x * lax.rsqrt(ms + EPS) * g_ref[...]).astype(BF16)
    z = _dot(h, wm_ref[...])
    xp_ref[...] = z[:, 0:D_POOL]
    hm = hm_ref[...]

    def headnorm(t, gain):
        sq = t * t
        hi = sq.astype(BF16)
        lo = (sq - hi.astype(F32)).astype(BF16)
        msh = _dot(hi, hm) + _dot(lo, hm)
        return t * lax.rsqrt(msh + EPS) * gain

    q_ref[...] = headnorm(z[:, D_POOL:D_POOL + D_ATTN], qg_ref[...])
    k_ref[...] = headnorm(z[:, D_POOL + D_ATTN:D_POOL + 2 * D_ATTN], kg_ref[...])
    v_ref[...] = z[:, D_POOL + 2 * D_ATTN:D_POOL + 3 * D_ATTN]
    f = _dot(h, wf_ref[...]) + bf_ref[...]
    lf_ref[...] = jnp.minimum(f, 0.0) - jnp.log1p(jnp.exp(-jnp.abs(f)))


def _inproj(x, g, wm, wf, bfp, qg, kg, hm):
    n = x.shape[0]
    t = min(TOK_TILE, n)
    row = lambda i: (i, 0)
    fix = lambda i: (0, 0)
    return pl.pallas_call(
        _inproj_body,
        grid=(n // t,),
        in_specs=[
            pl.BlockSpec((t, D_MODEL), row),
            pl.BlockSpec((1, D_MODEL), fix),
            pl.BlockSpec(wm.shape, fix),
            pl.BlockSpec(wf.shape, fix),
            pl.BlockSpec((1, LANES), fix),
            pl.BlockSpec((1, D_ATTN), fix),
            pl.BlockSpec((1, D_ATTN), fix),
            pl.BlockSpec(hm.shape, fix),
        ],
        out_specs=[
            pl.BlockSpec((t, D_POOL), row),
            pl.BlockSpec((t, D_ATTN), row),
            pl.BlockSpec((t, D_ATTN), row),
            pl.BlockSpec((t, D_ATTN), row),
            pl.BlockSpec((t, LANES), row),
        ],
        out_shape=[
            jax.ShapeDtypeStruct((n, D_POOL), F32),
            jax.ShapeDtypeStruct((n, D_ATTN), F32),
            jax.ShapeDtypeStruct((n, D_ATTN), F32),
            jax.ShapeDtypeStruct((n, D_ATTN), F32),
            jax.ShapeDtypeStruct((n, LANES), F32),
        ],
        compiler_params=_cparams(("arbitrary",)),
        name="inproj",
    )(x, g, wm, wf, bfp, qg, kg, hm)


def _pool_mix(xp, window_sum, cnt_fn, pw_ref, ps_ref):
    outs = []
    for g, w in enumerate(POOL_WINDOWS):
        lanes = slice(g * POOL_GROUP, (g + 1) * POOL_GROUP)
        pooled = window_sum(g, w) / cnt_fn(w) - xp[:, lanes]
        outs.append(_dot(pooled.astype(BF16), pw_ref[g]))
    return jnp.concatenate(outs, axis=1) * ps_ref[...]


def _poolprep_body(xp_ref, lf_ref, q_ref, k_ref, v_ref, pw_ref, ps_ref,
                   selq_ref, selk_ref, selv_ref,
                   po_ref, qa_ref, ka_ref, vt_ref, xx_ref, fc_ref):
    t = SEQ_TILE
    hist = 16
    step = pl.program_id(1)

    @pl.when(step == 0)
    def _():
        xx_ref[0:hist, :] = jnp.zeros((hist, D_POOL), F32)
        fc_ref[...] = jnp.zeros_like(fc_ref)

    xp = xp_ref[...]
    xx_ref[hist:hist + t, :] = xp
    pos = lax.broadcasted_iota(jnp.int32, (t, POOL_GROUP), 0) + step * t

    def window_sum(g, w):
        lanes = slice(g * POOL_GROUP, (g + 1) * POOL_GROUP)
        ws = xp[:, lanes]
        for r in range(1, w):
            ws = ws + xx_ref[hist - r:hist - r + t, lanes]
        return ws

    def cnt(w):
        return jnp.minimum(pos + 1, w).astype(F32)

    po_ref[...] = _pool_mix(xp, window_sum, cnt, pw_ref, ps_ref)
    xx_ref[0:hist, :] = xx_ref[t:t + hist, :]

    ri = lax.broadcasted_iota(jnp.int32, (t, t), 0)
    ci = lax.broadcasted_iota(jnp.int32, (t, t), 1)
    tri = jnp.where(ci <= ri, 1.0, 0.0).astype(BF16)
    hi, mid, lo = _split3(lf_ref[...])
    fcum = _dot(tri, hi) + _dot(tri, mid) + _dot(tri, lo) + fc_ref[...]
    fc_ref[...] = fcum[t - 1:t, :]
    fh, fm, fl = _split3(fcum)
    ones = jnp.ones((t, LANES), BF16)
    wq = jnp.concatenate([q_ref[...].astype(BF16), fh, fm, fl, ones], axis=1)
    wk = jnp.concatenate([k_ref[...].astype(BF16), fh, fm, fl, ones], axis=1)
    wv = jnp.concatenate([v_ref[...].astype(BF16), ones], axis=1)
    for hp in range(N_HEADS // 2):
        qa = _dot(wq, selq_ref[hp]).astype(BF16)
        ka = _dot(wk, selk_ref[hp]).astype(BF16)
        va = _dot_nt(selv_ref[hp], wv).astype(BF16)
        for e in range(2):
            qa_ref[2 * hp + e] = qa[:, e * AUG:(e + 1) * AUG]
            ka_ref[2 * hp + e] = ka[:, e * AUG:(e + 1) * AUG]
            vt_ref[2 * hp + e] = va[e * AUG:(e + 1) * AUG, :]


def _poolprep(xp, lf, q, k, v, pw, ps, selq, selk, selv, batch, seq):
    t = SEQ_TILE
    nt = seq // t
    n = batch * seq
    row = lambda b, i: (b * nt + i, 0)
    fix2 = lambda b, i: (0, 0)
    fix3 = lambda b, i: (0, 0, 0)
    return pl.pallas_call(
        _poolprep_body,
        grid=(batch, nt),
        in_specs=[
            pl.BlockSpec((t, D_POOL), row),
            pl.BlockSpec((t, LANES), row),
            pl.BlockSpec((t, D_ATTN), row),
            pl.BlockSpec((t, D_ATTN), row),
            pl.BlockSpec((t, D_ATTN), row),
            pl.BlockSpec(pw.shape, fix3),
            pl.BlockSpec((1, D_POOL), fix2),
            pl.BlockSpec(selq.shape, fix3),
            pl.BlockSpec(selk.shape, fix3),
            pl.BlockSpec(selv.shape, fix3),
        ],
        out_specs=[
            pl.BlockSpec((t, D_POOL), row),
            pl.BlockSpec((N_HEADS, t, AUG), lambda b, i: (0, b * nt + i, 0)),
            pl.BlockSpec((N_HEADS, t, AUG), lambda b, i: (0, b * nt + i, 0)),
            pl.BlockSpec((N_HEADS, None, AUG, t), lambda b, i: (0, b * nt + i, 0, 0)),
        ],
        out_shape=[
            jax.ShapeDtypeStruct((n, D_POOL), F32),
            jax.ShapeDtypeStruct((N_HEADS, n, AUG), BF16),
            jax.ShapeDtypeStruct((N_HEADS, n, AUG), BF16),
            jax.ShapeDtypeStruct((N_HEADS, n // t, AUG, t), BF16),
        ],
        scratch_shapes=[
            pltpu.VMEM((t + 16, D_POOL), F32),
            pltpu.VMEM((1, LANES), F32),
        ],
        compiler_params=_cparams(("arbitrary", "arbitrary")),
        name="poolprep",
    )(xp, lf, q, k, v, pw, ps, selq, selk, selv)


def _flash_body(qa_ref, ka_ref, vt_ref, o_ref, s0_scr, s1_scr):
    tq, tk = Q_TILE, KV_TILE
    i = pl.program_id(2)
    n_full = (i * tq) // tk
    qpos = lax.broadcasted_iota(jnp.int32, (tk, tq), 1) + i * tq
    krel = lax.broadcasted_iota(jnp.int32, (tk, tq), 0)
    vrows = V_ROWS

    s_slots = (s0_scr, s1_scr)

    def scores(j, slot):
        start = pl.multiple_of(j * tk, tk)
        for hh in range(2):
            s_slots[slot][hh] = _dot_nt(ka_ref[hh, pl.ds(start, tk), :], qa_ref[hh])

    def tile(j, slot, carry, masked):
        if not masked:
            scores(j + 1, 1 - slot)
        new = []
        for hh in range(2):
            m, acc = carry[hh]
            s = s_slots[slot][hh]
            if masked:
                s = jnp.where(krel + j * tk <= qpos, s, NEG)
            m_new = jnp.maximum(m, jnp.max(s, axis=0, keepdims=True))
            p = jnp.exp(s - m_new).astype(BF16)
            alpha = jnp.exp(m - m_new)
            acc = alpha * acc + _dot(vt_ref[hh, j, 0:vrows, :], p)
            new.append((m_new, acc))
        return tuple(new)

    def pair(jj, carry):
        carry = tile(2 * jj, 0, carry, False)
        return tile(2 * jj + 1, 1, carry, False)

    scores(0, 0)
    one = (jnp.full((1, tq), NEG, F32), jnp.zeros((vrows, tq), F32))
    carry = lax.fori_loop(0, n_full // 2, pair, (one, one))
    j0 = 2 * (n_full // 2)
    carry = lax.cond(
        n_full % 2 == 1,
        lambda c: tile(j0 + 1, 1, tile(j0, 0, c, False), True),
        lambda c: tile(j0, 0, c, True),
        carry)
    outs = []
    for hh in range(2):
        acc = jnp.concatenate([carry[hh][1], jnp.zeros((AUG - vrows, tq), F32)], axis=0)
        acc_t = acc.T
        outs.append(acc_t[:, 0:HEAD_DIM] / acc_t[:, HEAD_DIM:HEAD_DIM + 1])
    o_ref[...] = jnp.concatenate(outs, axis=1)


def _flash(qa, ka, vt, batch, seq):
    tq = Q_TILE
    nq = seq // tq
    n = batch * seq
    return pl.pallas_call(
        _flash_body,
        grid=(batch, N_HEADS // 2, nq),
        in_specs=[
            pl.BlockSpec((2, tq, AUG), lambda b, hp, i: (hp, b * nq + i, 0)),
            pl.BlockSpec((2, seq, AUG), lambda b, hp, i: (hp, b, 0)),
            pl.BlockSpec((2, seq // KV_TILE, AUG, KV_TILE), lambda b, hp, i: (hp, b, 0, 0)),
        ],
        out_specs=pl.BlockSpec((None, tq, 2 * HEAD_DIM), lambda b, hp, i: (hp, b * nq + i, 0)),
        out_shape=jax.ShapeDtypeStruct((N_HEADS // 2, n, 2 * HEAD_DIM), F32),
        scratch_shapes=[pltpu.VMEM((2, KV_TILE, Q_TILE), F32), pltpu.VMEM((2, KV_TILE, Q_TILE), F32)],
        compiler_params=_cparams(("arbitrary", "arbitrary", "arbitrary")),
        name="flash",
    )(qa, ka, vt)


def _sample_pool_body(xp_ref, st_ref, pw_ref, ps_ref, o_ref, *, start):
    xp = xp_ref[...]

    def window_sum(g, w):
        lanes = slice(g * POOL_GROUP, (g + 1) * POOL_GROUP)
        ws = xp[:, lanes]
        for r in range(1, w):
            ws = ws + st_ref[POOL_STATE - r][:, lanes]
        return ws

    o_ref[...] = _pool_mix(xp, window_sum, lambda w: float(min(start + 1, w)), pw_ref, ps_ref)


def _sample_pool(xp_s, state_t, pw, ps, start):
    return pl.pallas_call(
        functools.partial(_sample_pool_body, start=start),
        out_shape=jax.ShapeDtypeStruct(xp_s.shape, F32),
        compiler_params=pltpu.CompilerParams(vmem_limit_bytes=VMEM_LIMIT),
        name="sample_pool",
    )(xp_s, state_t, pw, ps)


def _sample_attn_body(pt_ref, q_ref, k_ref, v_ref, lf_ref, sfx_ref, *refs, n_steps):
    g_pages = PAGES_PER_STEP
    ck = refs[0:g_pages]
    cv = refs[g_pages:2 * g_pages]
    cl = refs[2 * g_pages:3 * g_pages]
    o_ref = refs[3 * g_pages]
    m_ref, l_ref, acc_ref, car_ref = refs[3 * g_pages + 1:]
    rows = 2 * N_HEADS
    step = pl.program_id(1)

    hrow = lax.broadcasted_iota(jnp.int32, (rows, D_ATTN), 0)
    hcol = lax.broadcasted_iota(jnp.int32, (rows, D_ATTN), 1) // HEAD_DIM
    headmask = hrow == hcol
    qf = q_ref[0].astype(BF16).astype(F32)
    qbd = jnp.where(headmask, qf * ATTN_SCALE, 0.0)

    @pl.when(step == 0)
    def _():
        m_ref[...] = jnp.full(m_ref.shape, NEG, F32)
        l_ref[...] = jnp.zeros_like(l_ref)
        acc_ref[...] = jnp.zeros_like(acc_ref)
        eye = (lax.broadcasted_iota(jnp.int32, (rows, LANES), 0)
               == lax.broadcasted_iota(jnp.int32, (rows, LANES), 1))
        c_new = jnp.sum(jnp.where(eye, lf_ref[0], 0.0), axis=1, keepdims=True)
        car_ref[...] = jnp.broadcast_to(c_new, car_ref.shape)

    carry = car_ref[...]
    sfx = sfx_ref[...]
    bias = []
    for g in range(g_pages):
        lft = cl[g][...]
        hi, mid, lo = _split3(jnp.concatenate([lft, jnp.zeros_like(lft)], axis=0))
        r = _dot(hi, sfx) + _dot(mid, sfx) + _dot(lo, sfx)
        bias.append(r[:, 0:PAGE_SIZE] + carry)
        carry = carry + r[:, PAGE_SIZE:2 * PAGE_SIZE]
    car_ref[...] = carry
    kt_all = jnp.concatenate(
        [ck[g][...].reshape(D_ATTN, PAGE_SIZE).astype(BF16) for g in range(g_pages)], axis=1)
    vt_all = jnp.concatenate(
        [cv[g][...].reshape(D_ATTN, PAGE_SIZE).astype(BF16) for g in range(g_pages)], axis=1)
    s_all = _dot(qbd.astype(BF16), kt_all) + jnp.concatenate(bias, axis=1)
    m_old = m_ref[...]
    m_new = jnp.maximum(m_old, jnp.max(s_all, axis=1, keepdims=True))
    p = jnp.exp(s_all - m_new)
    alpha = jnp.exp(m_old - m_new)
    l_ref[...] = alpha * l_ref[...] + jnp.sum(p, axis=1, keepdims=True)
    acc_ref[...] = alpha * acc_ref[...] + _dot_nt(p.astype(BF16), vt_all)
    m_ref[...] = m_new

    @pl.when(step == n_steps - 1)
    def _():
        kf = k_ref[0].astype(BF16).astype(F32)
        vf = v_ref[0].astype(BF16).astype(F32)
        s_new = jnp.sum(qbd * kf, axis=1, keepdims=True)
        m_old = m_ref[...]
        m_fin = jnp.maximum(m_old, s_new)
        a = jnp.exp(m_old - m_fin)
        p_new = jnp.exp(s_new - m_fin)
        l_fin = a * l_ref[...] + p_new
        acc = a * acc_ref[...] + p_new.astype(BF16).astype(F32) * vf
        o_ref[0] = jnp.sum(jnp.where(headmask, acc / l_fin, 0.0), axis=0, keepdims=True)


def _sample_attn(page_table, q_s, k_s, v_s, lf_s, sfx, cache_kt, cache_vt, cache_lft, layer):
    n_seq, n_pages = page_table.shape
    g_pages = PAGES_PER_STEP
    n_steps = n_pages // g_pages
    rows = 2 * N_HEADS

    def page_map(g, nd):
        def index(b, s, pt):
            return (layer, pt[b, n_pages - 1 - (s * g_pages + g)]) + (0,) * nd
        return index

    tok = lambda b, s, pt: (b, 0, 0)
    in_specs = [
        pl.BlockSpec((1, 1, D_ATTN), tok),
        pl.BlockSpec((1, 1, D_ATTN), tok),
        pl.BlockSpec((1, 1, D_ATTN), tok),
        pl.BlockSpec((1, 1, LANES), tok),
        pl.BlockSpec(sfx.shape, lambda b, s, pt: (0, 0)),
    ]
    page = (None, None, N_HEADS, HEAD_DIM, PAGE_SIZE)
    in_specs += [pl.BlockSpec(page, page_map(g, 3)) for g in range(g_pages)]
    in_specs += [pl.BlockSpec(page, page_map(g, 3)) for g in range(g_pages)]
    in_specs += [pl.BlockSpec((None, None, N_HEADS, PAGE_SIZE), page_map(g, 2)) for g in range(g_pages)]
    grid_spec = pltpu.PrefetchScalarGridSpec(
        num_scalar_prefetch=1,
        grid=(n_seq, n_steps),
        in_specs=in_specs,
        out_specs=pl.BlockSpec((1, 1, D_ATTN), tok),
        scratch_shapes=[
            pltpu.VMEM((rows, 1), F32),
            pltpu.VMEM((rows, 1), F32),
            pltpu.VMEM((rows, D_ATTN), F32),
            pltpu.VMEM((rows, LANES), F32),
        ],
    )
    return pl.pallas_call(
        functools.partial(_sample_attn_body, n_steps=n_steps),
        grid_spec=grid_spec,
        out_shape=jax.ShapeDtypeStruct((n_seq, 1, D_ATTN), F32),
        compiler_params=_cparams(("arbitrary", "arbitrary")),
        name="sample_attn",
    )(page_table, q_s, k_s, v_s, lf_s, sfx,
      *([cache_kt] * g_pages), *([cache_vt] * g_pages), *([cache_lft] * g_pages))


def _outproj_body(x_ref, po_ref, at_ref, wo_ref, g_ref, x1_ref, xn_ref):
    mix = jnp.concatenate([po_ref[...]] + [at_ref[hp] for hp in range(N_HEADS // 2)],
                          axis=1).astype(BF16)
    x1 = x_ref[...] + _dot(mix, wo_ref[...])
    x1_ref[...] = x1
    ms = jnp.mean(x1 * x1, axis=-1, keepdims=True)
    xn_ref[...] = (x1 * lax.rsqrt(ms + EPS) * g_ref[...]).astype(BF16)


def _outproj(x, po, at, wo, g):
    n = x.shape[0]
    t = min(TOK_TILE, n)
    row = lambda i: (i, 0)
    fix = lambda i: (0, 0)
    return pl.pallas_call(
        _outproj_body,
        grid=(n // t,),
        in_specs=[
            pl.BlockSpec((t, D_MODEL), row),
            pl.BlockSpec((t, D_POOL), row),
            pl.BlockSpec((N_HEADS // 2, t, 2 * HEAD_DIM), lambda i: (0, i, 0)),
            pl.BlockSpec(wo.shape, fix),
            pl.BlockSpec((1, D_MODEL), fix),
        ],
        out_specs=[pl.BlockSpec((t, D_MODEL), row), pl.BlockSpec((t, D_MODEL), row)],
        out_shape=[jax.ShapeDtypeStruct((n, D_MODEL), F32), jax.ShapeDtypeStruct((n, D_MODEL), BF16)],
        compiler_params=_cparams(("arbitrary",)),
        name="outproj",
    )(x, po, at, wo, g)


def _top16_ranks(s, sv_ref, half, rowi):
    rank = jnp.full(s.shape, float(PEER_TOPK), F32)
    for c in range(PEER_TOPK):
        m = jnp.max(s, axis=0, keepdims=True)
        first = jnp.min(jnp.where(s == m, rowi, float(N_KEYS)), axis=0, keepdims=True)
        sel = rowi == first
        rank = jnp.where(sel, float(c), rank)
        s = jnp.where(sel, -jnp.inf, s)
        sv_ref[half, c:c + 1, :] = m
    return rank


def _top16_ranks_distinct(halves, sv_ref):
    ss = list(halves)
    ranks = [jnp.full(s.shape, float(PEER_TOPK), F32) for s in ss]
    for c in range(PEER_TOPK):
        for half in range(len(ss)):
            m = jnp.max(ss[half], axis=0, keepdims=True)
            sel = ss[half] == m
            ranks[half] = jnp.where(sel, float(c), ranks[half])
            ss[half] = jnp.where(sel, -jnp.inf, ss[half])
            sv_ref[half, c:c + 1, :] = m
    return ranks


def _pair_segments():
    segs, row = [], 0
    for c in range(PEER_TOPK):
        nd = PEER_TOPK // (c + 1)
        if nd < 8:
            break
        segs.append((row, c, nd))
        row += nd
    bins = []
    for c in range(len(segs), PEER_TOPK):
        nd = PEER_TOPK // (c + 1)
        for bn in bins:
            if bn[0] + nd <= 8:
                bn[1].append((bn[0], c, nd))
                bn[0] += nd
                break
        else:
            bins.append([nd, [(0, c, nd)]])
    for bn in bins:
        segs += [(row + o, c, nd) for o, c, nd in bn[1]]
        row += 8
    return segs, row


def _peer_route_body(xn_ref, wq_ref, sk_ref, flat_ref, a_ref, n_ref, b_ref, r_ref,
                     q_scr, sv_ref, rk_ref, cand_scr, wgt_scr):
    t = xn_ref.shape[0]
    q = _dot(xn_ref[...], wq_ref[...]).astype(BF16)
    for hp in range(2 * PEER_HEADS):
        q_scr[hp] = q[:, hp * D_HALF:(hp + 1) * D_HALF]
    rowi = lax.broadcasted_iota(jnp.int32, (N_KEYS, t), 0).astype(F32)
    segs, n_rows = _pair_segments()
    flat = flat_ref[...]
    big = float(PEER_TOPK * PEER_TOPK)
    cand_scr[...] = jnp.full(cand_scr.shape, -jnp.inf, F32)
    wgt_scr[...] = jnp.zeros_like(wgt_scr)

    def head(h, _):
        s1 = _dot_nt(sk_ref[0], q_scr[2 * h])
        s2 = _dot_nt(sk_ref[1], q_scr[2 * h + 1])
        rk_ref[0], rk_ref[1] = _top16_ranks_distinct((s1, s2), sv_ref)
        ranked = (jnp.sum(jnp.where(rk_ref[0] < float(PEER_TOPK), 1.0, 0.0), axis=0, keepdims=True)
                  + jnp.sum(jnp.where(rk_ref[1] < float(PEER_TOPK), 1.0, 0.0), axis=0, keepdims=True))

        @pl.when(jnp.max(jnp.abs(ranked - float(2 * PEER_TOPK))) > 0.0)
        def _():
            rk_ref[0] = _top16_ranks(s1, sv_ref, 0, rowi)
            rk_ref[1] = _top16_ranks(s2, sv_ref, 1, rowi)

        r1 = rk_ref[0]
        r2 = rk_ref[1]
        sv1 = sv_ref[0]
        sv2 = sv_ref[1]
        e1 = jnp.exp(sv1 - sv1[0:1, :])
        e2 = jnp.exp(sv2 - sv2[0:1, :])
        for row, c, nd in segs:
            cand_scr[row:row + nd, :] = sv1[c:c + 1, :] + sv2[0:nd, :]
            wgt_scr[row:row + nd, :] = e1[c:c + 1, :] * e2[0:nd, :]
        cand = cand_scr[...]
        chosen = jnp.zeros(cand.shape, F32)
        for _k in range(PEER_TOPK):
            m = jnp.max(cand, axis=0, keepdims=True)
            first = jnp.min(jnp.where(cand == m, flat, big), axis=0, keepdims=True)
            sel = flat == first
            chosen = jnp.where(sel, 1.0, chosen)
            cand = jnp.where(sel, -jnp.inf, cand)
        z = jnp.sum(chosen * wgt_scr[...], axis=0, keepdims=True)
        nrow = jnp.zeros((N_KEYS, t), F32)
        for row, c, nd in segs:
            n_c = jnp.sum(chosen[row:row + nd, :], axis=0, keepdims=True)
            nrow = jnp.where(r1 == float(c), n_c, nrow)
        a_ref[h] = jnp.where(r1 < float(PEER_TOPK), jnp.exp(s1 - sv1[0:1, :]) / z, 0.0)
        n_ref[h] = nrow
        b_ref[h] = jnp.exp(s2 - sv2[0:1, :]).astype(BF16)
        r_ref[h] = r2.astype(BF16)
        return 0

    lax.fori_loop(0, PEER_HEADS, head, 0)


def _peer_route(xn, wq, sk):
    n = xn.shape[0]
    t = ROUTE_TILE
    fac = pl.BlockSpec((None, PEER_HEADS, N_KEYS, t), lambda i: (i, 0, 0, 0))
    shp = jax.ShapeDtypeStruct((n // t, PEER_HEADS, N_KEYS, t), F32)
    shp_bf = jax.ShapeDtypeStruct((n // t, PEER_HEADS, N_KEYS, t), BF16)
    segs, n_rows = _pair_segments()
    flat = np.full((n_rows, t), 1e9, np.float32)
    for row, c, nd in segs:
        flat[row:row + nd, :] = (c * PEER_TOPK + np.arange(nd, dtype=np.float32))[:, None]
    return pl.pallas_call(
        _peer_route_body,
        grid=(n // t,),
        in_specs=[
            pl.BlockSpec((t, D_MODEL), lambda i: (i, 0)),
            pl.BlockSpec(wq.shape, lambda i: (0, 0)),
            pl.BlockSpec(sk.shape, lambda i: (0, 0, 0)),
            pl.BlockSpec(flat.shape, lambda i: (0, 0)),
        ],
        out_specs=[fac, fac, fac, fac],
        out_shape=[shp, shp, shp_bf, shp_bf],
        scratch_shapes=[
            pltpu.VMEM((2 * PEER_HEADS, t, D_HALF), BF16),
            pltpu.VMEM((2, PEER_TOPK, t), F32),
            pltpu.VMEM((2, N_KEYS, t), F32),
            pltpu.VMEM((n_rows, t), F32),
            pltpu.VMEM((n_rows, t), F32),
        ],
        compiler_params=_cparams(("arbitrary",)),
        name="peer_route",
    )(xn, wq, sk, jnp.asarray(flat))


def _peer_expert_body(xn_ref, a_ref, n_ref, b_ref, r_ref, u_ref, vt_ref, x1_ref,
                      o_ref, acc_ref, act0_ref, act1_ref, *, n_blocks):
    s = pl.program_id(1)
    acts = (act0_ref, act1_ref)

    def project(slot):
        acts[slot][...] = _dot_nt(u_ref[...], xn_ref[...])

    def combine(slot):
        a_rows = a_ref[...].astype(BF16)
        n_rows = n_ref[...].astype(BF16)
        zero = jnp.zeros((), BF16)
        rtile = a_ref.shape[-1]
        parts = []
        for i in range(EXP_ROWS):
            row = []
            for k in range(a_ref.shape[0]):
                act = acts[slot][i * N_KEYS:(i + 1) * N_KEYS, k * rtile:(k + 1) * rtile]
                ge = jax.nn.gelu(act.astype(BF16), approximate=True)
                gate = None
                for h in range(PEER_HEADS):
                    keep = r_ref[k, h] < n_rows[k, h, i:i + 1, :]
                    term = jnp.where(keep, b_ref[k, h], zero) * a_rows[k, h, i:i + 1, :]
                    gate = term if gate is None else gate + term
                row.append(gate * ge)
            parts.append(jnp.concatenate(row, axis=1))
        w = jnp.concatenate(parts, axis=0)
        acc_ref[...] += _dot(vt_ref[...], w)

    first = s == 0
    last = s == n_blocks
    even = s % 2 == 0

    @pl.when(first)
    def _():
        acc_ref[...] = jnp.zeros_like(acc_ref)
        project(0)

    @pl.when(jnp.logical_and(jnp.logical_not(first), jnp.logical_and(jnp.logical_not(last), even)))
    def _():
        project(0)
        combine(1)

    @pl.when(jnp.logical_and(jnp.logical_not(last), jnp.logical_not(even)))
    def _():
        project(1)
        combine(0)

    @pl.when(last)
    def _():
        combine((n_blocks - 1) % 2)
        o_ref[...] = x1_ref[...] + acc_ref[...].T


def _peer_expert(xn, fa, fn, fb, fr, u, vt, x1):
    n = xn.shape[0]
    t = min(TOK_TILE, n)
    eb = EXP_ROWS * N_KEYS
    n_blocks = u.shape[0] // eb
    rt = t // ROUTE_TILE
    tok = lambda i, s: (i, 0)
    proj = lambda s: jnp.minimum(s, n_blocks - 1)
    comb = lambda s: jnp.maximum(s - 1, 0)
    return pl.pallas_call(
        functools.partial(_peer_expert_body, n_blocks=n_blocks),
        grid=(n // t, n_blocks + 1),
        in_specs=[
            pl.BlockSpec((t, D_MODEL), tok),
            pl.BlockSpec((rt, PEER_HEADS, EXP_ROWS, ROUTE_TILE), lambda i, s: (i, 0, comb(s), 0)),
            pl.BlockSpec((rt, PEER_HEADS, EXP_ROWS, ROUTE_TILE), lambda i, s: (i, 0, comb(s), 0)),
            pl.BlockSpec((rt, PEER_HEADS, N_KEYS, ROUTE_TILE), lambda i, s: (i, 0, 0, 0)),
            pl.BlockSpec((rt, PEER_HEADS, N_KEYS, ROUTE_TILE), lambda i, s: (i, 0, 0, 0)),
            pl.BlockSpec((eb, D_MODEL), lambda i, s: (proj(s), 0)),
            pl.BlockSpec((None, D_MODEL, eb), lambda i, s: (comb(s), 0, 0)),
            pl.BlockSpec((t, D_MODEL), tok),
        ],
        out_specs=pl.BlockSpec((t, D_MODEL), tok),
        out_shape=jax.ShapeDtypeStruct((n, D_MODEL), F32),
        scratch_shapes=[pltpu.VMEM((D_MODEL, t), F32), pltpu.VMEM((eb, t), F32),
                        pltpu.VMEM((eb, t), F32)],
        compiler_params=_cparams(("arbitrary", "arbitrary")),
        name="peer_expert",
    )(xn, fa, fn, fb, fr, u, vt, x1)


def _selectors():
    wide = D_ATTN + 4 * LANES
    selq = np.zeros((N_HEADS, wide, AUG), np.float32)
    selk = np.zeros((N_HEADS, wide, AUG), np.float32)
    selv = np.zeros((N_HEADS, AUG, D_ATTN + LANES), np.float32)
    ones_row = D_ATTN + 3 * LANES
    for h in range(N_HEADS):
        for d in range(HEAD_DIM):
            selq[h, h * HEAD_DIM + d, d] = ATTN_SCALE
            selk[h, h * HEAD_DIM + d, d] = 1.0
            selv[h, d, h * HEAD_DIM + d] = 1.0
        for piece in range(3):
            selq[h, D_ATTN + piece * LANES + h, HEAD_DIM + piece] = 1.0
            selq[h, ones_row, HEAD_DIM + 3 + piece] = 1.0
            selk[h, ones_row, HEAD_DIM + piece] = 1.0
            selk[h, D_ATTN + piece * LANES + h, HEAD_DIM + 3 + piece] = -1.0
        selv[h, HEAD_DIM, D_ATTN] = 1.0
    hp = N_HEADS // 2
    selq = selq.reshape(hp, 2, wide, AUG).transpose(0, 2, 1, 3).reshape(hp, wide, 2 * AUG)
    selk = selk.reshape(hp, 2, wide, AUG).transpose(0, 2, 1, 3).reshape(hp, wide, 2 * AUG)
    selv = selv.reshape(hp, 2 * AUG, D_ATTN + LANES)
    return (jnp.asarray(selq, BF16), jnp.asarray(selk, BF16), jnp.asarray(selv, BF16))


def kernel(x_prompt, x_sample, cache_k, cache_v, cache_logf, state_pool, page_table,
           norm_mix, w_in, b_forget, q_norm, k_norm, pool_w, pool_scale, w_out,
           norm_ffn, peer_wq, peer_subkeys, peer_u, peer_v):
    batch, seq, _ = x_prompt.shape
    n_seq, dec_seq, _ = x_sample.shape
    depth = w_in.shape[0]
    n_pool = cache_k.shape[1]
    n_pages = page_table.shape[1]
    past = n_pages * PAGE_SIZE
    assert dec_seq == 1 and seq % KV_TILE == 0 and n_pages % PAGES_PER_STEP == 0
    assert SEQ_TILE == KV_TILE and TOK_TILE % ROUTE_TILE == 0
    assert n_seq % 8 == 0
    n_prompt = batch * seq
    assert n_prompt % TOK_TILE == 0
    s_tile = ROUTE_TILE if n_seq <= ROUTE_TILE else TOK_TILE
    n_samp = -(-n_seq // s_tile) * s_tile
    hp = N_HEADS // 2

    def pad_rows(t, axis=0):
        widths = [(0, 0)] * t.ndim
        widths[axis] = (0, n_samp - n_seq)
        return jnp.pad(t, widths)

    x_p = x_prompt.reshape(n_prompt, D_MODEL)
    x_s = pad_rows(x_sample.reshape(n_seq, D_MODEL))
    selq, selk, selv = _selectors()
    hm = jnp.asarray(np.kron(np.eye(N_HEADS), np.full((HEAD_DIM, HEAD_DIM), 1.0 / HEAD_DIM)), BF16)
    jj = np.arange(PAGE_SIZE)
    sfx = jnp.asarray(np.concatenate([(jj[:, None] > jj[None, :]).astype(np.float32),
                                      np.ones((PAGE_SIZE, PAGE_SIZE), np.float32)], axis=1), BF16)
    cache_kt = jnp.transpose(cache_k, (0, 1, 3, 4, 2))
    cache_vt = jnp.transpose(cache_v, (0, 1, 3, 4, 2))
    cache_lft = jnp.transpose(cache_logf, (0, 1, 3, 2))

    kp_l, vp_l, fp_l, pp_l, ks_l, vs_l, fs_l, ps_l = [], [], [], [], [], [], [], []
    for l in range(depth):
        wm = w_in[l][:, :D_POOL + 3 * D_ATTN].astype(BF16)
        wf = jnp.pad(w_in[l][:, D_POOL + 3 * D_ATTN:], ((0, 0), (0, LANES - N_HEADS))).astype(BF16)
        bfp = jnp.pad(b_forget[l], (0, LANES - N_HEADS)).reshape(1, LANES)
        inproj = lambda t: _inproj(t, norm_mix[l].reshape(1, D_MODEL), wm, wf, bfp,
                                   jnp.tile(q_norm[l], N_HEADS).reshape(1, D_ATTN),
                                   jnp.tile(k_norm[l], N_HEADS).reshape(1, D_ATTN), hm)
        xp, q, k, v, lf = inproj(x_p)
        xp_s, q_s, k_s, v_s, lf_s = (t[:n_seq] for t in inproj(x_s))
        pw = pool_w[l].astype(BF16)
        ps = pool_scale[l].reshape(1, D_POOL)

        po_p, qa, ka, vt = _poolprep(xp, lf, q, k, v, pw, ps, selq, selk, selv, batch, seq)
        at_p = _flash(qa, ka, vt, batch, seq)

        po_s = _sample_pool(xp_s, jnp.transpose(state_pool[l], (1, 0, 2)), pw, ps, past)
        row = lambda t: t.reshape(n_seq, 1, t.shape[1])
        at_s = _sample_attn(page_table, row(q_s), row(k_s), row(v_s), row(lf_s), sfx,
                            cache_kt, cache_vt, cache_lft, l).reshape(n_seq, hp, 2 * HEAD_DIM)
        at_s = pad_rows(jnp.transpose(at_s, (1, 0, 2)), axis=1)

        wo = w_out[l].astype(BF16)
        gf = norm_ffn[l].reshape(1, D_MODEL)
        wq = peer_wq[l].astype(BF16)
        sk = peer_subkeys[l].astype(BF16)
        u = peer_u[l].astype(BF16)
        eb = EXP_ROWS * N_KEYS
        vt_blocks = jnp.transpose(peer_v[l].reshape(-1, eb, D_MODEL), (0, 2, 1)).astype(BF16)

        def channel_mix(x_in, po, at):
            x1, xn = _outproj(x_in, po, at, wo, gf)
            fa, fn, fb, fr = _peer_route(xn, wq, sk)
            return _peer_expert(xn, fa, fn, fb, fr, u, vt_blocks, x1)

        x_p = channel_mix(x_p, po_p, at_p)
        x_s = channel_mix(x_s, pad_rows(po_s), at_s)

        kp_l.append(k.reshape(batch, seq, N_HEADS, HEAD_DIM))
        vp_l.append(v.reshape(batch, seq, N_HEADS, HEAD_DIM))
        fp_l.append(lf[:, :N_HEADS].reshape(batch, seq, N_HEADS))
        pp_l.append(xp.reshape(batch, seq, D_POOL)[:, seq - POOL_STATE:])
        ks_l.append(k_s.reshape(n_seq, 1, N_HEADS, HEAD_DIM))
        vs_l.append(v_s.reshape(n_seq, 1, N_HEADS, HEAD_DIM))
        fs_l.append(lf_s[:, :N_HEADS].reshape(n_seq, 1, N_HEADS))
        ps_l.append(jnp.concatenate([state_pool[l][:, 1:], xp_s[:, None, :]], axis=1))

    return (x_p.reshape(batch, seq, D_MODEL), x_s[:n_seq].reshape(n_seq, 1, D_MODEL),
            jnp.stack(kp_l), jnp.stack(vp_l), jnp.stack(fp_l), jnp.stack(pp_l),
            jnp.stack(ks_l), jnp.stack(vs_l), jnp.stack(fs_l), jnp.stack(ps_l))
```

```python
import functools

import jax
import jax.numpy as jnp
import numpy as np
from jax import lax
from jax.experimental import pallas as pl
from jax.experimental.pallas import tpu as pltpu

F32 = jnp.float32
BF16 = jnp.bfloat16

EPS = 1e-6
D_MODEL = 1024
D_POOL = 512
POOL_WINDOWS = (2, 4, 8, 16)
POOL_GROUP = 128
POOL_STATE = 15
N_HEADS = 8
HEAD_DIM = 64
D_ATTN = N_HEADS * HEAD_DIM
PAGE_SIZE = 128
PEER_HEADS = 8
PEER_TOPK = 16
N_KEYS = 128
D_HALF = 128
ATTN_SCALE = HEAD_DIM ** -0.5

LANES = 128
AUG = 128
V_ROWS = 80
TOK_TILE = 512
SEQ_TILE = 512
ROUTE_TILE = 256
Q_TILE = 256
KV_TILE = 512
EXP_ROWS = 8
PAGES_PER_STEP = 16
VMEM_LIMIT = 56 * 1024 * 1024
NEG = -1e30


def _cparams(sem):
    return pltpu.CompilerParams(dimension_semantics=sem, vmem_limit_bytes=VMEM_LIMIT)


def _split3(x):
    hi = x.astype(BF16)
    r = x - hi.astype(F32)
    mid = r.astype(BF16)
    lo = (r - mid.astype(F32)).astype(BF16)
    return hi, mid, lo


def _dot(a, b):
    return jnp.dot(a, b, preferred_element_type=F32)


def _dot_nt(a, b):
    return lax.dot_general(a, b, (((1,), (1,)), ((), ())), preferred_element_type=F32)


def _inproj_body(x_ref, g_ref, wm_ref, wf_ref, bf_ref, qg_ref, kg_ref, hm_ref,
                 xp_ref, q_ref, k_ref, v_ref, lf_ref):
    x = x_ref[...]
    ms = jnp.mean(x * x, axis=-1, keepdims=True)
    h = (x * lax.rsqrt(ms + EPS) * g_ref[...]).astype(BF16)
    z = _dot(h, wm_ref[...])
    xp_ref[...] = z[:, 0:D_POOL]
    hm = hm_ref[...]

    def headnorm(t, gain):
        sq = t * t
        hi = sq.astype(BF16)
        lo = (sq - hi.astype(F32)).astype(BF16)
        msh = _dot(hi, hm) + _dot(lo, hm)
        return t * lax.rsqrt(msh + EPS) * gain

    q_ref[...] = headnorm(z[:, D_POOL:D_POOL + D_ATTN], qg_ref[...])
    k_ref[...] = headnorm(z[:, D_POOL + D_ATTN:D_POOL + 2 * D_ATTN], kg_ref[...])
    v_ref[...] = z[:, D_POOL + 2 * D_ATTN:D_POOL + 3 * D_ATTN]
    f = _dot(h, wf_ref[...]) + bf_ref[...]
    lf_ref[...] = jnp.minimum(f, 0.0) - jnp.log1p(jnp.exp(-jnp.abs(f)))


def _inproj(x, g, wm, wf, bfp, qg, kg, hm):
    n = x.shape[0]
    t = min(TOK_TILE, n)
    row = lambda i: (i, 0)
    fix = lambda i: (0, 0)
    return pl.pallas_call(
        _inproj_body,
        grid=(n // t,),
        in_specs=[
            pl.BlockSpec((t, D_MODEL), row),
            pl.BlockSpec((1, D_MODEL), fix),
            pl.BlockSpec(wm.shape, fix),
            pl.BlockSpec(wf.shape, fix),
            pl.BlockSpec((1, LANES), fix),
            pl.BlockSpec((1, D_ATTN), fix),
            pl.BlockSpec((1, D_ATTN), fix),
            pl.BlockSpec(hm.shape, fix),
        ],
        out_specs=[
            pl.BlockSpec((t, D_POOL), row),
            pl.BlockSpec((t, D_ATTN), row),
            pl.BlockSpec((t, D_ATTN), row),
            pl.BlockSpec((t, D_ATTN), row),
            pl.BlockSpec((t, LANES), row),
        ],
        out_shape=[
            jax.ShapeDtypeStruct((n, D_POOL), F32),
            jax.ShapeDtypeStruct((n, D_ATTN), F32),
            jax.ShapeDtypeStruct((n, D_ATTN), F32),
            jax.ShapeDtypeStruct((n, D_ATTN), F32),
            jax.ShapeDtypeStruct((n, LANES), F32),
        ],
        compiler_params=_cparams(("arbitrary",)),
        name="inproj",
    )(x, g, wm, wf, bfp, qg, kg, hm)


def _pool_mix(xp, window_sum, cnt_fn, pw_ref, ps_ref):
    outs = []
    for g, w in enumerate(POOL_WINDOWS):
        lanes = slice(g * POOL_GROUP, (g + 1) * POOL_GROUP)
        pooled = window_sum(g, w) / cnt_fn(w) - xp[:, lanes]
        outs.append(_dot(pooled.astype(BF16), pw_ref[g]))
    return jnp.concatenate(outs, axis=1) * ps_ref[...]


def _poolprep_body(xp_ref, lf_ref, q_ref, k_ref, v_ref, pw_ref, ps_ref,
                   selq_ref, selk_ref, selv_ref,
                   po_ref, qa_ref, ka_ref, vt_ref, xx_ref, fc_ref):
    t = SEQ_TILE
    hist = 16
    step = pl.program_id(1)

    @pl.when(step == 0)
    def _():
        xx_ref[0:hist, :] = jnp.zeros((hist, D_POOL), F32)
        fc_ref[...] = jnp.zeros_like(fc_ref)

    xp = xp_ref[...]
    xx_ref[hist:hist + t, :] = xp
    pos = lax.broadcasted_iota(jnp.int32, (t, POOL_GROUP), 0) + step * t

    def window_sum(g, w):
        lanes = slice(g * POOL_GROUP, (g + 1) * POOL_GROUP)
        ws = xp[:, lanes]
        for r in range(1, w):
            ws = ws + xx_ref[hist - r:hist - r + t, lanes]
        return ws

    def cnt(w):
        return jnp.minimum(pos + 1, w).astype(F32)

    po_ref[...] = _pool_mix(xp, window_sum, cnt, pw_ref, ps_ref)
    xx_ref[0:hist, :] = xx_ref[t:t + hist, :]

    ri = lax.broadcasted_iota(jnp.int32, (t, t), 0)
    ci = lax.broadcasted_iota(jnp.int32, (t, t), 1)
    tri = jnp.where(ci <= ri, 1.0, 0.0).astype(BF16)
    hi, mid, lo = _split3(lf_ref[...])
    fcum = _dot(tri, hi) + _dot(tri, mid) + _dot(tri, lo) + fc_ref[...]
    fc_ref[...] = fcum[t - 1:t, :]
    fh, fm, fl = _split3(fcum)
    ones = jnp.ones((t, LANES), BF16)
    wq = jnp.concatenate([q_ref[...].astype(BF16), fh, fm, fl, ones], axis=1)
    wk = jnp.concatenate([k_ref[...].astype(BF16), fh, fm, fl, ones], axis=1)
    wv = jnp.concatenate([v_ref[...].astype(BF16), ones], axis=1)
    for hp in range(N_HEADS // 2):
        qa = _dot(wq, selq_ref[hp]).astype(BF16)
        ka = _dot(wk, selk_ref[hp]).astype(BF16)
        va = _dot_nt(selv_ref[hp], wv).astype(BF16)
        for e in range(2):
            qa_ref[2 * hp + e] = qa[:, e * AUG:(e + 1) * AUG]
            ka_ref[2 * hp + e] = ka[:, e * AUG:(e + 1) * AUG]
            vt_ref[2 * hp + e] = va[e * AUG:(e + 1) * AUG, :]


def _poolprep(xp, lf, q, k, v, pw, ps, selq, selk, selv, batch, seq):
    t = SEQ_TILE
    nt = seq // t
    n = batch * seq
    row = lambda b, i: (b * nt + i, 0)
    fix2 = lambda b, i: (0, 0)
    fix3 = lambda b, i: (0, 0, 0)
    return pl.pallas_call(
        _poolprep_body,
        grid=(batch, nt),
        in_specs=[
            pl.BlockSpec((t, D_POOL), row),
            pl.BlockSpec((t, LANES), row),
            pl.BlockSpec((t, D_ATTN), row),
            pl.BlockSpec((t, D_ATTN), row),
            pl.BlockSpec((t, D_ATTN), row),
            pl.BlockSpec(pw.shape, fix3),
            pl.BlockSpec((1, D_POOL), fix2),
            pl.BlockSpec(selq.shape, fix3),
            pl.BlockSpec(selk.shape, fix3),
            pl.BlockSpec(selv.shape, fix3),
        ],
        out_specs=[
            pl.BlockSpec((t, D_POOL), row),
            pl.BlockSpec((N_HEADS, t, AUG), lambda b, i: (0, b * nt + i, 0)),
            pl.BlockSpec((N_HEADS, t, AUG), lambda b, i: (0, b * nt + i, 0)),
            pl.BlockSpec((N_HEADS, None, AUG, t), lambda b, i: (0, b * nt + i, 0, 0)),
        ],
        out_shape=[
            jax.ShapeDtypeStruct((n, D_POOL), F32),
            jax.ShapeDtypeStruct((N_HEADS, n, AUG), BF16),
            jax.ShapeDtypeStruct((N_HEADS, n, AUG), BF16),
            jax.ShapeDtypeStruct((N_HEADS, n // t, AUG, t), BF16),
        ],
        scratch_shapes=[
            pltpu.VMEM((t + 16, D_POOL), F32),
            pltpu.VMEM((1, LANES), F32),
        ],
        compiler_params=_cparams(("arbitrary", "arbitrary")),
        name="poolprep",
    )(xp, lf, q, k, v, pw, ps, selq, selk, selv)


def _flash_body(qa_ref, ka_ref, vt_ref, o_ref, s0_scr, s1_scr):
    tq, tk = Q_TILE, KV_TILE
    i = pl.program_id(2)
    n_full = (i * tq) // tk
    qpos = lax.broadcasted_iota(jnp.int32, (tk, tq), 1) + i * tq
    krel = lax.broadcasted_iota(jnp.int32, (tk, tq), 0)
    vrows = V_ROWS

    s_slots = (s0_scr, s1_scr)

    def scores(j, slot):
        start = pl.multiple_of(j * tk, tk)
        for hh in range(2):
            s_slots[slot][hh] = _dot_nt(ka_ref[hh, pl.ds(start, tk), :], qa_ref[hh])

    def tile(j, slot, carry, masked):
        if not masked:
            scores(j + 1, 1 - slot)
        new = []
        for hh in range(2):
            m, acc = carry[hh]
            s = s_slots[slot][hh]
            if masked:
                s = jnp.where(krel + j * tk <= qpos, s, NEG)
            m_new = jnp.maximum(m, jnp.max(s, axis=0, keepdims=True))
            p = jnp.exp(s - m_new).astype(BF16)
            alpha = jnp.exp(m - m_new)
            acc = alpha * acc + _dot(vt_ref[hh, j, 0:vrows, :], p)
            new.append((m_new, acc))
        return tuple(new)

    def pair(jj, carry):
        carry = tile(2 * jj, 0, carry, False)
        return tile(2 * jj + 1, 1, carry, False)

    scores(0, 0)
    one = (jnp.full((1, tq), NEG, F32), jnp.zeros((vrows, tq), F32))
    carry = lax.fori_loop(0, n_full // 2, pair, (one, one))
    j0 = 2 * (n_full // 2)
    carry = lax.cond(
        n_full % 2 == 1,
        lambda c: tile(j0 + 1, 1, tile(j0, 0, c, False), True),
        lambda c: tile(j0, 0, c, True),
        carry)
    outs = []
    for hh in range(2):
        acc = jnp.concatenate([carry[hh][1], jnp.zeros((AUG - vrows, tq), F32)], axis=0)
        acc_t = acc.T
        outs.append(acc_t[:, 0:HEAD_DIM] / acc_t[:, HEAD_DIM:HEAD_DIM + 1])
    o_ref[...] = jnp.concatenate(outs, axis=1)


def _flash(qa, ka, vt, batch, seq):
    tq = Q_TILE
    nq = seq // tq
    n = batch * seq
    return pl.pallas_call(
        _flash_body,
        grid=(batch, N_HEADS // 2, nq),
        in_specs=[
            pl.BlockSpec((2, tq, AUG), lambda b, hp, i: (hp, b * nq + i, 0)),
            pl.BlockSpec((2, seq, AUG), lambda b, hp, i: (hp, b, 0)),
            pl.BlockSpec((2, seq // KV_TILE, AUG, KV_TILE), lambda b, hp, i: (hp, b, 0, 0)),
        ],
        out_specs=pl.BlockSpec((None, tq, 2 * HEAD_DIM), lambda b, hp, i: (hp, b * nq + i, 0)),
        out_shape=jax.ShapeDtypeStruct((N_HEADS // 2, n, 2 * HEAD_DIM), F32),
        scratch_shapes=[pltpu.VMEM((2, KV_TILE, Q_TILE), F32), pltpu.VMEM((2, KV_TILE, Q_TILE), F32)],
        compiler_params=_cparams(("arbitrary", "arbitrary", "arbitrary")),
        name="flash",
    )(qa, ka, vt)


def _sample_pool_body(xp_ref, st_ref, pw_ref, ps_ref, o_ref, *, start):
    xp = xp_ref[...]

    def window_sum(g, w):
        lanes = slice(g * POOL_GROUP, (g + 1) * POOL_GROUP)
        ws = xp[:, lanes]
        for r in range(1, w):
            ws = ws + st_ref[POOL_STATE - r][:, lanes]
        return ws

    o_ref[...] = _pool_mix(xp, window_sum, lambda w: float(min(start + 1, w)), pw_ref, ps_ref)


def _sample_pool(xp_s, state_t, pw, ps, start):
    return pl.pallas_call(
        functools.partial(_sample_pool_body, start=start),
        out_shape=jax.ShapeDtypeStruct(xp_s.shape, F32),
        compiler_params=pltpu.CompilerParams(vmem_limit_bytes=VMEM_LIMIT),
        name="sample_pool",
    )(xp_s, state_t, pw, ps)


def _sample_attn_body(pt_ref, q_ref, k_ref, v_ref, lf_ref, sfx_ref, *refs, n_steps):
    g_pages = PAGES_PER_STEP
    ck = refs[0:g_pages]
    cv = refs[g_pages:2 * g_pages]
    cl = refs[2 * g_pages:3 * g_pages]
    o_ref = refs[3 * g_pages]
    m_ref, l_ref, acc_ref, car_ref = refs[3 * g_pages + 1:]
    rows = 2 * N_HEADS
    step = pl.program_id(1)

    hrow = lax.broadcasted_iota(jnp.int32, (rows, D_ATTN), 0)
    hcol = lax.broadcasted_iota(jnp.int32, (rows, D_ATTN), 1) // HEAD_DIM
    headmask = hrow == hcol
    qf = q_ref[0].astype(BF16).astype(F32)
    qbd = jnp.where(headmask, qf * ATTN_SCALE, 0.0)

    @pl.when(step == 0)
    def _():
        m_ref[...] = jnp.full(m_ref.shape, NEG, F32)
        l_ref[...] = jnp.zeros_like(l_ref)
        acc_ref[...] = jnp.zeros_like(acc_ref)
        eye = (lax.broadcasted_iota(jnp.int32, (rows, LANES), 0)
               == lax.broadcasted_iota(jnp.int32, (rows, LANES), 1))
        c_new = jnp.sum(jnp.where(eye, lf_ref[0], 0.0), axis=1, keepdims=True)
        car_ref[...] = jnp.broadcast_to(c_new, car_ref.shape)

    carry = car_ref[...]
    sfx = sfx_ref[...]
    bias = []
    for g in range(g_pages):
        lft = cl[g][...]
        hi, mid, lo = _split3(jnp.concatenate([lft, jnp.zeros_like(lft)], axis=0))
        r = _dot(hi, sfx) + _dot(mid, sfx) + _dot(lo, sfx)
        bias.append(r[:, 0:PAGE_SIZE] + carry)
        carry = carry + r[:, PAGE_SIZE:2 * PAGE_SIZE]
    car_ref[...] = carry
    kt_all = jnp.concatenate(
        [ck[g][...].reshape(D_ATTN, PAGE_SIZE).astype(BF16) for g in range(g_pages)], axis=1)
    vt_all = jnp.concatenate(
        [cv[g][...].reshape(D_ATTN, PAGE_SIZE).astype(BF16) for g in range(g_pages)], axis=1)
    s_all = _dot(qbd.astype(BF16), kt_all) + jnp.concatenate(bias, axis=1)
    m_old = m_ref[...]
    m_new = jnp.maximum(m_old, jnp.max(s_all, axis=1, keepdims=True))
    p = jnp.exp(s_all - m_new)
    alpha = jnp.exp(m_old - m_new)
    l_ref[...] = alpha * l_ref[...] + jnp.sum(p, axis=1, keepdims=True)
    acc_ref[...] = alpha * acc_ref[...] + _dot_nt(p.astype(BF16), vt_all)
    m_ref[...] = m_new

    @pl.when(step == n_steps - 1)
    def _():
        kf = k_ref[0].astype(BF16).astype(F32)
        vf = v_ref[0].astype(BF16).astype(F32)
        s_new = jnp.sum(qbd * kf, axis=1, keepdims=True)
        m_old = m_ref[...]
        m_fin = jnp.maximum(m_old, s_new)
        a = jnp.exp(m_old - m_fin)
        p_new = jnp.exp(s_new - m_fin)
        l_fin = a * l_ref[...] + p_new
        acc = a * acc_ref[...] + p_new.astype(BF16).astype(F32) * vf
        o_ref[0] = jnp.sum(jnp.where(headmask, acc / l_fin, 0.0), axis=0, keepdims=True)


def _sample_attn(page_table, q_s, k_s, v_s, lf_s, sfx, cache_kt, cache_vt, cache_lft, layer):
    n_seq, n_pages = page_table.shape
    g_pages = PAGES_PER_STEP
    n_steps = n_pages // g_pages
    rows = 2 * N_HEADS

    def page_map(g, nd):
        def index(b, s, pt):
            return (layer, pt[b, n_pages - 1 - (s * g_pages + g)]) + (0,) * nd
        return index

    tok = lambda b, s, pt: (b, 0, 0)
    in_specs = [
        pl.BlockSpec((1, 1, D_ATTN), tok),
        pl.BlockSpec((1, 1, D_ATTN), tok),
        pl.BlockSpec((1, 1, D_ATTN), tok),
        pl.BlockSpec((1, 1, LANES), tok),
        pl.BlockSpec(sfx.shape, lambda b, s, pt: (0, 0)),
    ]
    page = (None, None, N_HEADS, HEAD_DIM, PAGE_SIZE)
    in_specs += [pl.BlockSpec(page, page_map(g, 3)) for g in range(g_pages)]
    in_specs += [pl.BlockSpec(page, page_map(g, 3)) for g in range(g_pages)]
    in_specs += [pl.BlockSpec((None, None, N_HEADS, PAGE_SIZE), page_map(g, 2)) for g in range(g_pages)]
    grid_spec = pltpu.PrefetchScalarGridSpec(
        num_scalar_prefetch=1,
        grid=(n_seq, n_steps),
        in_specs=in_specs,
        out_specs=pl.BlockSpec((1, 1, D_ATTN), tok),
        scratch_shapes=[
            pltpu.VMEM((rows, 1), F32),
            pltpu.VMEM((rows, 1), F32),
            pltpu.VMEM((rows, D_ATTN), F32),
            pltpu.VMEM((rows, LANES), F32),
        ],
    )
    return pl.pallas_call(
        functools.partial(_sample_attn_body, n_steps=n_steps),
        grid_spec=grid_spec,
        out_shape=jax.ShapeDtypeStruct((n_seq, 1, D_ATTN), F32),
        compiler_params=_cparams(("arbitrary", "arbitrary")),
        name="sample_attn",
    )(page_table, q_s, k_s, v_s, lf_s, sfx,
      *([cache_kt] * g_pages), *([cache_vt] * g_pages), *([cache_lft] * g_pages))


def _outproj_body(x_ref, po_ref, at_ref, wo_ref, g_ref, x1_ref, xn_ref):
    mix = jnp.concatenate([po_ref[...]] + [at_ref[hp] for hp in range(N_HEADS // 2)],
                          axis=1).astype(BF16)
    x1 = x_ref[...] + _dot(mix, wo_ref[...])
    x1_ref[...] = x1
    ms = jnp.mean(x1 * x1, axis=-1, keepdims=True)
    xn_ref[...] = (x1 * lax.rsqrt(ms + EPS) * g_ref[...]).astype(BF16)


def _outproj(x, po, at, wo, g):
    n = x.shape[0]
    t = min(TOK_TILE, n)
    row = lambda i: (i, 0)
    fix = lambda i: (0, 0)
    return pl.pallas_call(
        _outproj_body,
        grid=(n // t,),
        in_specs=[
            pl.BlockSpec((t, D_MODEL), row),
            pl.BlockSpec((t, D_POOL), row),
            pl.BlockSpec((N_HEADS // 2, t, 2 * HEAD_DIM), lambda i: (0, i, 0)),
            pl.BlockSpec(wo.shape, fix),
            pl.BlockSpec((1, D_MODEL), fix),
        ],
        out_specs=[pl.BlockSpec((t, D_MODEL), row), pl.BlockSpec((t, D_MODEL), row)],
        out_shape=[jax.ShapeDtypeStruct((n, D_MODEL), F32), jax.ShapeDtypeStruct((n, D_MODEL), BF16)],
        compiler_params=_cparams(("arbitrary",)),
        name="outproj",
    )(x, po, at, wo, g)


def _top16_ranks(s, sv_ref, half, rowi):
    rank = jnp.full(s.shape, float(PEER_TOPK), F32)
    for c in range(PEER_TOPK):
        m = jnp.max(s, axis=0, keepdims=True)
        first = jnp.min(jnp.where(s == m, rowi, float(N_KEYS)), axis=0, keepdims=True)
        sel = rowi == first
        rank = jnp.where(sel, float(c), rank)
        s = jnp.where(sel, -jnp.inf, s)
        sv_ref[half, c:c + 1, :] = m
    return rank


def _top16_ranks_distinct(halves, sv_ref):
    ss = list(halves)
    ranks = [jnp.full(s.shape, float(PEER_TOPK), F32) for s in ss]
    for c in range(PEER_TOPK):
        for half in range(len(ss)):
            m = jnp.max(ss[half], axis=0, keepdims=True)
            sel = ss[half] == m
            ranks[half] = jnp.where(sel, float(c), ranks[half])
            ss[half] = jnp.where(sel, -jnp.inf, ss[half])
            sv_ref[half, c:c + 1, :] = m
    return ranks


def _pair_segments():
    segs, row = [], 0
    for c in range(PEER_TOPK):
        nd = PEER_TOPK // (c + 1)
        if nd < 8:
            break
        segs.append((row, c, nd))
        row += nd
    bins = []
    for c in range(len(segs), PEER_TOPK):
        nd = PEER_TOPK // (c + 1)
        for bn in bins:
            if bn[0] + nd <= 8:
                bn[1].append((bn[0], c, nd))
                bn[0] += nd
                break
        else:
            bins.append([nd, [(0, c, nd)]])
    for bn in bins:
        segs += [(row + o, c, nd) for o, c, nd in bn[1]]
        row += 8
    return segs, row


def _peer_route_body(xn_ref, wq_ref, sk_ref, flat_ref, a_ref, n_ref, b_ref, r_ref,
                     q_scr, sv_ref, rk_ref, cand_scr, wgt_scr):
    t = xn_ref.shape[0]
    q = _dot(xn_ref[...], wq_ref[...]).astype(BF16)
    for hp in range(2 * PEER_HEADS):
        q_scr[hp] = q[:, hp * D_HALF:(hp + 1) * D_HALF]
    rowi = lax.broadcasted_iota(jnp.int32, (N_KEYS, t), 0).astype(F32)
    segs, n_rows = _pair_segments()
    flat = flat_ref[...]
    big = float(PEER_TOPK * PEER_TOPK)
    cand_scr[...] = jnp.full(cand_scr.shape, -jnp.inf, F32)
    wgt_scr[...] = jnp.zeros_like(wgt_scr)

    def head(h, _):
        s1 = _dot_nt(sk_ref[0], q_scr[2 * h])
        s2 = _dot_nt(sk_ref[1], q_scr[2 * h + 1])
        rk_ref[0], rk_ref[1] = _top16_ranks_distinct((s1, s2), sv_ref)
        ranked = (jnp.sum(jnp.where(rk_ref[0] < float(PEER_TOPK), 1.0, 0.0), axis=0, keepdims=True)
                  + jnp.sum(jnp.where(rk_ref[1] < float(PEER_TOPK), 1.0, 0.0), axis=0, keepdims=True))

        @pl.when(jnp.max(jnp.abs(ranked - float(2 * PEER_TOPK))) > 0.0)
        def _():
            rk_ref[0] = _top16_ranks(s1, sv_ref, 0, rowi)
            rk_ref[1] = _top16_ranks(s2, sv_ref, 1, rowi)

        r1 = rk_ref[0]
        r2 = rk_ref[1]
        sv1 = sv_ref[0]
        sv2 = sv_ref[1]
        e1 = jnp.exp(sv1 - sv1[0:1, :])
        e2 = jnp.exp(sv2 - sv2[0:1, :])
        for row, c, nd in segs:
            cand_scr[row:row + nd, :] = sv1[c:c + 1, :] + sv2[0:nd, :]
            wgt_scr[row:row + nd, :] = e1[c:c + 1, :] * e2[0:nd, :]
        cand = cand_scr[...]
        chosen = jnp.zeros(cand.shape, F32)
        for _k in range(PEER_TOPK):
            m = jnp.max(cand, axis=0, keepdims=True)
            first = jnp.min(jnp.where(cand == m, flat, big), axis=0, keepdims=True)
            sel = flat == first
            chosen = jnp.where(sel, 1.0, chosen)
            cand = jnp.where(sel, -jnp.inf, cand)
        z = jnp.sum(chosen * wgt_scr[...], axis=0, keepdims=True)
        nrow = jnp.zeros((N_KEYS, t), F32)
        for row, c, nd in segs:
            n_c = jnp.sum(chosen[row:row + nd, :], axis=0, keepdims=True)
            nrow = jnp.where(r1 == float(c), n_c, nrow)
        a_ref[h] = jnp.where(r1 < float(PEER_TOPK), jnp.exp(s1 - sv1[0:1, :]) / z, 0.0)
        n_ref[h] = nrow
        b_ref[h] = jnp.exp(s2 - sv2[0:1, :]).astype(BF16)
        r_ref[h] = r2.astype(BF16)
        return 0

    lax.fori_loop(0, PEER_HEADS, head, 0)


def _peer_route(xn, wq, sk):
    n = xn.shape[0]
    t = ROUTE_TILE
    fac = pl.BlockSpec((None, PEER_HEADS, N_KEYS, t), lambda i: (i, 0, 0, 0))
    shp = jax.ShapeDtypeStruct((n // t, PEER_HEADS, N_KEYS, t), F32)
    shp_bf = jax.ShapeDtypeStruct((n // t, PEER_HEADS, N_KEYS, t), BF16)
    segs, n_rows = _pair_segments()
    flat = np.full((n_rows, t), 1e9, np.float32)
    for row, c, nd in segs:
        flat[row:row + nd, :] = (c * PEER_TOPK + np.arange(nd, dtype=np.float32))[:, None]
    return pl.pallas_call(
        _peer_route_body,
        grid=(n // t,),
        in_specs=[
            pl.BlockSpec((t, D_MODEL), lambda i: (i, 0)),
            pl.BlockSpec(wq.shape, lambda i: (0, 0)),
            pl.BlockSpec(sk.shape, lambda i: (0, 0, 0)),
            pl.BlockSpec(flat.shape, lambda i: (0, 0)),
        ],
        out_specs=[fac, fac, fac, fac],
        out_shape=[shp, shp, shp_bf, shp_bf],
        scratch_shapes=[
            pltpu.VMEM((2 * PEER_HEADS, t, D_HALF), BF16),
            pltpu.VMEM((2, PEER_TOPK, t), F32),
            pltpu.VMEM((2, N_KEYS, t), F32),
            pltpu.VMEM((n_rows, t), F32),
            pltpu.VMEM((n_rows, t), F32),
        ],
        compiler_params=_cparams(("arbitrary",)),
        name="peer_route",
    )(xn, wq, sk, jnp.asarray(flat))


def _peer_expert_body(xn_ref, a_ref, n_ref, b_ref, r_ref, u_ref, vt_ref, x1_ref,
                      o_ref, acc_ref, act0_ref, act1_ref, *, n_blocks):
    s = pl.program_id(1)
    acts = (act0_ref, act1_ref)

    def project(slot):
        acts[slot][...] = _dot_nt(u_ref[...], xn_ref[...])

    def combine(slot):
        a_rows = a_ref[...].astype(BF16)
        n_rows = n_ref[...].astype(BF16)
        zero = jnp.zeros((), BF16)
        rtile = a_ref.shape[-1]
        parts = []
        for i in range(EXP_ROWS):
            row = []
            for k in range(a_ref.shape[0]):
                act = acts[slot][i * N_KEYS:(i + 1) * N_KEYS, k * rtile:(k + 1) * rtile]
                ge = jax.nn.gelu(act.astype(BF16), approximate=True)
                gate = None
                for h in range(PEER_HEADS):
                    keep = r_ref[k, h] < n_rows[k, h, i:i + 1, :]
                    term = jnp.where(keep, b_ref[k, h], zero) * a_rows[k, h, i:i + 1, :]
                    gate = term if gate is None else gate + term
                row.append(gate * ge)
            parts.append(jnp.concatenate(row, axis=1))
        w = jnp.concatenate(parts, axis=0)
        acc_ref[...] += _dot(vt_ref[...], w)

    first = s == 0
    last = s == n_blocks
    even = s % 2 == 0

    @pl.when(first)
    def _():
        acc_ref[...] = jnp.zeros_like(acc_ref)
        project(0)

    @pl.when(jnp.logical_and(jnp.logical_not(first), jnp.logical_and(jnp.logical_not(last), even)))
    def _():
        project(0)
        combine(1)

    @pl.when(jnp.logical_and(jnp.logical_not(last), jnp.logical_not(even)))
    def _():
        project(1)
        combine(0)

    @pl.when(last)
    def _():
        combine((n_blocks - 1) % 2)
        o_ref[...] = x1_ref[...] + acc_ref[...].T


def _peer_expert(xn, fa, fn, fb, fr, u, vt, x1, layer):
    n = xn.shape[0]
    t = min(TOK_TILE, n)
    eb = EXP_ROWS * N_KEYS
    n_blocks = u.shape[1] // eb
    rt = t // ROUTE_TILE
    tok = lambda i, s: (i, 0)
    proj = lambda s: jnp.minimum(s, n_blocks - 1)
    comb = lambda s: jnp.maximum(s - 1, 0)
    return pl.pallas_call(
        functools.partial(_peer_expert_body, n_blocks=n_blocks),
        grid=(n // t, n_blocks + 1),
        in_specs=[
            pl.BlockSpec((t, D_MODEL), tok),
            pl.BlockSpec((rt, PEER_HEADS, EXP_ROWS, ROUTE_TILE), lambda i, s: (i, 0, comb(s), 0)),
            pl.BlockSpec((rt, PEER_HEADS, EXP_ROWS, ROUTE_TILE), lambda i, s: (i, 0, comb(s), 0)),
            pl.BlockSpec((rt, PEER_HEADS, N_KEYS, ROUTE_TILE), lambda i, s: (i, 0, 0, 0)),
            pl.BlockSpec((rt, PEER_HEADS, N_KEYS, ROUTE_TILE), lambda i, s: (i, 0, 0, 0)),
            pl.BlockSpec((None, eb, D_MODEL), lambda i, s: (layer, proj(s), 0)),
            pl.BlockSpec((None, None, D_MODEL, eb), lambda i, s: (layer, comb(s), 0, 0)),
            pl.BlockSpec((t, D_MODEL), tok),
        ],
        out_specs=pl.BlockSpec((t, D_MODEL), tok),
        out_shape=jax.ShapeDtypeStruct((n, D_MODEL), F32),
        scratch_shapes=[pltpu.VMEM((D_MODEL, t), F32), pltpu.VMEM((eb, t), F32),
                        pltpu.VMEM((eb, t), F32)],
        compiler_params=_cparams(("arbitrary", "arbitrary")),
        name="peer_expert",
    )(xn, fa, fn, fb, fr, u, vt, x1)


def _selectors():
    wide = D_ATTN + 4 * LANES
    selq = np.zeros((N_HEADS, wide, AUG), np.float32)
    selk = np.zeros((N_HEADS, wide, AUG), np.float32)
    selv = np.zeros((N_HEADS, AUG, D_ATTN + LANES), np.float32)
    ones_row = D_ATTN + 3 * LANES
    for h in range(N_HEADS):
        for d in range(HEAD_DIM):
            selq[h, h * HEAD_DIM + d, d] = ATTN_SCALE
            selk[h, h * HEAD_DIM + d, d] = 1.0
            selv[h, d, h * HEAD_DIM + d] = 1.0
        for piece in range(3):
            selq[h, D_ATTN + piece * LANES + h, HEAD_DIM + piece] = 1.0
            selq[h, ones_row, HEAD_DIM + 3 + piece] = 1.0
            selk[h, ones_row, HEAD_DIM + piece] = 1.0
            selk[h, D_ATTN + piece * LANES + h, HEAD_DIM + 3 + piece] = -1.0
        selv[h, HEAD_DIM, D_ATTN] = 1.0
    hp = N_HEADS // 2
    selq = selq.reshape(hp, 2, wide, AUG).transpose(0, 2, 1, 3).reshape(hp, wide, 2 * AUG)
    selk = selk.reshape(hp, 2, wide, AUG).transpose(0, 2, 1, 3).reshape(hp, wide, 2 * AUG)
    selv = selv.reshape(hp, 2 * AUG, D_ATTN + LANES)
    return (jnp.asarray(selq, BF16), jnp.asarray(selk, BF16), jnp.asarray(selv, BF16))


def kernel(x_prompt, x_sample, cache_k, cache_v, cache_logf, state_pool, page_table,
           norm_mix, w_in, b_forget, q_norm, k_norm, pool_w, pool_scale, w_out,
           norm_ffn, peer_wq, peer_subkeys, peer_u, peer_v):
    batch, seq, _ = x_prompt.shape
    n_seq, dec_seq, _ = x_sample.shape
    depth = w_in.shape[0]
    n_pool = cache_k.shape[1]
    n_pages = page_table.shape[1]
    past = n_pages * PAGE_SIZE
    assert dec_seq == 1 and seq % KV_TILE == 0 and n_pages % PAGES_PER_STEP == 0
    assert SEQ_TILE == KV_TILE and TOK_TILE % ROUTE_TILE == 0
    assert n_seq % 8 == 0
    n_prompt = batch * seq
    assert n_prompt % TOK_TILE == 0
    s_tile = ROUTE_TILE if n_seq <= ROUTE_TILE else TOK_TILE
    n_samp = -(-n_seq // s_tile) * s_tile
    hp = N_HEADS // 2

    def pad_rows(t, axis=0):
        widths = [(0, 0)] * t.ndim
        widths[axis] = (0, n_samp - n_seq)
        return jnp.pad(t, widths)

    x_p = x_prompt.reshape(n_prompt, D_MODEL)
    x_s = pad_rows(x_sample.reshape(n_seq, D_MODEL))
    selq, selk, selv = _selectors()
    hm = jnp.asarray(np.kron(np.eye(N_HEADS), np.full((HEAD_DIM, HEAD_DIM), 1.0 / HEAD_DIM)), BF16)
    jj = np.arange(PAGE_SIZE)
    sfx = jnp.asarray(np.concatenate([(jj[:, None] > jj[None, :]).astype(np.float32),
                                      np.ones((PAGE_SIZE, PAGE_SIZE), np.float32)], axis=1), BF16)
    cache_kt = jnp.transpose(cache_k, (0, 1, 3, 4, 2))
    cache_vt = jnp.transpose(cache_v, (0, 1, 3, 4, 2))
    cache_lft = jnp.transpose(cache_logf, (0, 1, 3, 2))

    eb = EXP_ROWS * N_KEYS
    u_all = peer_u.astype(BF16)
    vt_all = jnp.transpose(peer_v.reshape(depth, -1, eb, D_MODEL), (0, 1, 3, 2)).astype(BF16)

    kp_l, vp_l, fp_l, pp_l, ks_l, vs_l, fs_l, ps_l = [], [], [], [], [], [], [], []
    for l in range(depth):
        wm = w_in[l][:, :D_POOL + 3 * D_ATTN].astype(BF16)
        wf = jnp.pad(w_in[l][:, D_POOL + 3 * D_ATTN:], ((0, 0), (0, LANES - N_HEADS))).astype(BF16)
        bfp = jnp.pad(b_forget[l], (0, LANES - N_HEADS)).reshape(1, LANES)
        inproj = lambda t: _inproj(t, norm_mix[l].reshape(1, D_MODEL), wm, wf, bfp,
                                   jnp.tile(q_norm[l], N_HEADS).reshape(1, D_ATTN),
                                   jnp.tile(k_norm[l], N_HEADS).reshape(1, D_ATTN), hm)
        xp, q, k, v, lf = inproj(x_p)
        xp_s, q_s, k_s, v_s, lf_s = (t[:n_seq] for t in inproj(x_s))
        pw = pool_w[l].astype(BF16)
        ps = pool_scale[l].reshape(1, D_POOL)

        po_p, qa, ka, vt = _poolprep(xp, lf, q, k, v, pw, ps, selq, selk, selv, batch, seq)
        at_p = _flash(qa, ka, vt, batch, seq)

        po_s = _sample_pool(xp_s, jnp.transpose(state_pool[l], (1, 0, 2)), pw, ps, past)
        row = lambda t: t.reshape(n_seq, 1, t.shape[1])
        at_s = _sample_attn(page_table, row(q_s), row(k_s), row(v_s), row(lf_s), sfx,
                            cache_kt, cache_vt, cache_lft, l).reshape(n_seq, hp, 2 * HEAD_DIM)
        at_s = pad_rows(jnp.transpose(at_s, (1, 0, 2)), axis=1)

        wo = w_out[l].astype(BF16)
        gf = norm_ffn[l].reshape(1, D_MODEL)
        wq = peer_wq[l].astype(BF16)
        sk = peer_subkeys[l].astype(BF16)

        def channel_mix(x_in, po, at, layer=l):
            x1, xn = _outproj(x_in, po, at, wo, gf)
            fa, fn, fb, fr = _peer_route(xn, wq, sk)
            return _peer_expert(xn, fa, fn, fb, fr, u_all, vt_all, x1, layer)

        x_p = channel_mix(x_p, po_p, at_p)
        x_s = channel_mix(x_s, pad_rows(po_s), at_s)

        kp_l.append(k.reshape(batch, seq, N_HEADS, HEAD_DIM))
        vp_l.append(v.reshape(batch, seq, N_HEADS, HEAD_DIM))
        fp_l.append(lf[:, :N_HEADS].reshape(batch, seq, N_HEADS))
        pp_l.append(xp.reshape(batch, seq, D_POOL)[:, seq - POOL_STATE:])
        ks_l.append(k_s.reshape(n_seq, 1, N_HEADS, HEAD_DIM))
        vs_l.append(v_s.reshape(n_seq, 1, N_HEADS, HEAD_DIM))
        fs_l.append(lf_s[:, :N_HEADS].reshape(n_seq, 1, N_HEADS))
        ps_l.append(jnp.concatenate([state_pool[l][:, 1:], xp_s[:, None, :]], axis=1))

    return (x_p.reshape(batch, seq, D_MODEL), x_s[:n_seq].reshape(n_seq, 1, D_MODEL),
            jnp.stack(kp_l), jnp.stack(vp_l), jnp.stack(fp_l), jnp.stack(pp_l),
            jnp.stack(ks_l), jnp.stack(vs_l), jnp.stack(fs_l), jnp.stack(ps_l))
```

```python
import functools

import jax
import jax.numpy as jnp
import numpy as np
from jax import lax
from jax.experimental import pallas as pl
from jax.experimental.pallas import tpu as pltpu

F32 = jnp.float32
BF16 = jnp.bfloat16

EPS = 1e-6
D_MODEL = 1024
D_POOL = 512
POOL_WINDOWS = (2, 4, 8, 16)
POOL_GROUP = 128
POOL_STATE = 15
N_HEADS = 8
HEAD_DIM = 64
D_ATTN = N_HEADS * HEAD_DIM
PAGE_SIZE = 128
PEER_HEADS = 8
PEER_TOPK = 16
N_KEYS = 128
D_HALF = 128
ATTN_SCALE = HEAD_DIM ** -0.5

LANES = 128
AUG = 128
V_ROWS = 80
TOK_TILE = 512
SEQ_TILE = 512
ROUTE_TILE = 256
Q_TILE = 256
KV_TILE = 512
EXP_ROWS = 8
PAGES_PER_STEP = 32
VMEM_LIMIT = 56 * 1024 * 1024
NEG = -1e30


def _cparams(sem):
    return pltpu.CompilerParams(dimension_semantics=sem, vmem_limit_bytes=VMEM_LIMIT)


def _split3(x):
    hi = x.astype(BF16)
    r = x - hi.astype(F32)
    mid = r.astype(BF16)
    lo = (r - mid.astype(F32)).astype(BF16)
    return hi, mid, lo


def _dot(a, b):
    return jnp.dot(a, b, preferred_element_type=F32)


def _dot_nt(a, b):
    return lax.dot_general(a, b, (((1,), (1,)), ((), ())), preferred_element_type=F32)


def _inproj_body(x_ref, g_ref, wm_ref, wf_ref, bf_ref, qg_ref, kg_ref, hm_ref,
                 xp_ref, q_ref, k_ref, v_ref, lf_ref):
    x = x_ref[...]
    ms = jnp.mean(x * x, axis=-1, keepdims=True)
    h = (x * lax.rsqrt(ms + EPS) * g_ref[...]).astype(BF16)
    z = _dot(h, wm_ref[...])
    xp_ref[...] = z[:, 0:D_POOL]
    hm = hm_ref[...]

    def headnorm(t, gain):
        sq = t * t
        hi = sq.astype(BF16)
        lo = (sq - hi.astype(F32)).astype(BF16)
        msh = _dot(hi, hm) + _dot(lo, hm)
        return t * lax.rsqrt(msh + EPS) * gain

    q_ref[...] = headnorm(z[:, D_POOL:D_POOL + D_ATTN], qg_ref[...])
    k_ref[...] = headnorm(z[:, D_POOL + D_ATTN:D_POOL + 2 * D_ATTN], kg_ref[...])
    v_ref[...] = z[:, D_POOL + 2 * D_ATTN:D_POOL + 3 * D_ATTN]
    f = _dot(h, wf_ref[...]) + bf_ref[...]
    lf_ref[...] = jnp.minimum(f, 0.0) - jnp.log1p(jnp.exp(-jnp.abs(f)))


def _inproj(x, g, wm, wf, bfp, qg, kg, hm):
    n = x.shape[0]
    t = min(TOK_TILE, n)
    row = lambda i: (i, 0)
    fix = lambda i: (0, 0)
    return pl.pallas_call(
        _inproj_body,
        grid=(n // t,),
        in_specs=[
            pl.BlockSpec((t, D_MODEL), row),
            pl.BlockSpec((1, D_MODEL), fix),
            pl.BlockSpec(wm.shape, fix),
            pl.BlockSpec(wf.shape, fix),
            pl.BlockSpec((1, LANES), fix),
            pl.BlockSpec((1, D_ATTN), fix),
            pl.BlockSpec((1, D_ATTN), fix),
            pl.BlockSpec(hm.shape, fix),
        ],
        out_specs=[
            pl.BlockSpec((t, D_POOL), row),
            pl.BlockSpec((t, D_ATTN), row),
            pl.BlockSpec((t, D_ATTN), row),
            pl.BlockSpec((t, D_ATTN), row),
            pl.BlockSpec((t, LANES), row),
        ],
        out_shape=[
            jax.ShapeDtypeStruct((n, D_POOL), F32),
            jax.ShapeDtypeStruct((n, D_ATTN), F32),
            jax.ShapeDtypeStruct((n, D_ATTN), F32),
            jax.ShapeDtypeStruct((n, D_ATTN), F32),
            jax.ShapeDtypeStruct((n, LANES), F32),
        ],
        compiler_params=_cparams(("arbitrary",)),
        name="inproj",
    )(x, g, wm, wf, bfp, qg, kg, hm)


def _pool_mix(xp, window_sum, cnt_fn, pw_ref, ps_ref):
    outs = []
    for g, w in enumerate(POOL_WINDOWS):
        lanes = slice(g * POOL_GROUP, (g + 1) * POOL_GROUP)
        pooled = window_sum(g, w) / cnt_fn(w) - xp[:, lanes]
        outs.append(_dot(pooled.astype(BF16), pw_ref[g]))
    return jnp.concatenate(outs, axis=1) * ps_ref[...]


def _poolprep_body(xp_ref, lf_ref, q_ref, k_ref, v_ref, pw_ref, ps_ref,
                   selq_ref, selk_ref, selv_ref,
                   po_ref, qa_ref, ka_ref, vt_ref, xx_ref, fc_ref):
    t = SEQ_TILE
    hist = 16
    step = pl.program_id(1)

    @pl.when(step == 0)
    def _():
        xx_ref[0:hist, :] = jnp.zeros((hist, D_POOL), F32)
        fc_ref[...] = jnp.zeros_like(fc_ref)

    xp = xp_ref[...]
    xx_ref[hist:hist + t, :] = xp
    pos = lax.broadcasted_iota(jnp.int32, (t, POOL_GROUP), 0) + step * t

    def window_sum(g, w):
        lanes = slice(g * POOL_GROUP, (g + 1) * POOL_GROUP)
        ws = xp[:, lanes]
        for r in range(1, w):
            ws = ws + xx_ref[hist - r:hist - r + t, lanes]
        return ws

    def cnt(w):
        return jnp.minimum(pos + 1, w).astype(F32)

    po_ref[...] = _pool_mix(xp, window_sum, cnt, pw_ref, ps_ref)
    xx_ref[0:hist, :] = xx_ref[t:t + hist, :]

    ri = lax.broadcasted_iota(jnp.int32, (t, t), 0)
    ci = lax.broadcasted_iota(jnp.int32, (t, t), 1)
    tri = jnp.where(ci <= ri, 1.0, 0.0).astype(BF16)
    hi, mid, lo = _split3(lf_ref[...])
    fcum = _dot(tri, hi) + _dot(tri, mid) + _dot(tri, lo) + fc_ref[...]
    fc_ref[...] = fcum[t - 1:t, :]
    fh, fm, fl = _split3(fcum)
    ones = jnp.ones((t, LANES), BF16)
    wq = jnp.concatenate([q_ref[...].astype(BF16), fh, fm, fl, ones], axis=1)
    wk = jnp.concatenate([k_ref[...].astype(BF16), fh, fm, fl, ones], axis=1)
    wv = jnp.concatenate([v_ref[...].astype(BF16), ones], axis=1)
    for hp in range(N_HEADS // 2):
        qa = _dot(wq, selq_ref[hp]).astype(BF16)
        ka = _dot(wk, selk_ref[hp]).astype(BF16)
        va = _dot_nt(selv_ref[hp], wv).astype(BF16)
        for e in range(2):
            qa_ref[2 * hp + e] = qa[:, e * AUG:(e + 1) * AUG]
            ka_ref[2 * hp + e] = ka[:, e * AUG:(e + 1) * AUG]
            vt_ref[2 * hp + e] = va[e * AUG:(e + 1) * AUG, :]


def _poolprep(xp, lf, q, k, v, pw, ps, selq, selk, selv, batch, seq):
    t = SEQ_TILE
    nt = seq // t
    n = batch * seq
    row = lambda b, i: (b * nt + i, 0)
    fix2 = lambda b, i: (0, 0)
    fix3 = lambda b, i: (0, 0, 0)
    return pl.pallas_call(
        _poolprep_body,
        grid=(batch, nt),
        in_specs=[
            pl.BlockSpec((t, D_POOL), row),
            pl.BlockSpec((t, LANES), row),
            pl.BlockSpec((t, D_ATTN), row),
            pl.BlockSpec((t, D_ATTN), row),
            pl.BlockSpec((t, D_ATTN), row),
            pl.BlockSpec(pw.shape, fix3),
            pl.BlockSpec((1, D_POOL), fix2),
            pl.BlockSpec(selq.shape, fix3),
            pl.BlockSpec(selk.shape, fix3),
            pl.BlockSpec(selv.shape, fix3),
        ],
        out_specs=[
            pl.BlockSpec((t, D_POOL), row),
            pl.BlockSpec((N_HEADS, t, AUG), lambda b, i: (0, b * nt + i, 0)),
            pl.BlockSpec((N_HEADS, t, AUG), lambda b, i: (0, b * nt + i, 0)),
            pl.BlockSpec((N_HEADS, None, AUG, t), lambda b, i: (0, b * nt + i, 0, 0)),
        ],
        out_shape=[
            jax.ShapeDtypeStruct((n, D_POOL), F32),
            jax.ShapeDtypeStruct((N_HEADS, n, AUG), BF16),
            jax.ShapeDtypeStruct((N_HEADS, n, AUG), BF16),
            jax.ShapeDtypeStruct((N_HEADS, n // t, AUG, t), BF16),
        ],
        scratch_shapes=[
            pltpu.VMEM((t + 16, D_POOL), F32),
            pltpu.VMEM((1, LANES), F32),
        ],
        compiler_params=_cparams(("arbitrary", "arbitrary")),
        name="poolprep",
    )(xp, lf, q, k, v, pw, ps, selq, selk, selv)


def _flash_body(qa_ref, ka_ref, vt_ref, o_ref, s0_scr, s1_scr):
    tq, tk = Q_TILE, KV_TILE
    i = pl.program_id(2)
    n_full = (i * tq) // tk
    qpos = lax.broadcasted_iota(jnp.int32, (tk, tq), 1) + i * tq
    krel = lax.broadcasted_iota(jnp.int32, (tk, tq), 0)
    vrows = V_ROWS

    s_slots = (s0_scr, s1_scr)

    def scores(j, slot):
        start = pl.multiple_of(j * tk, tk)
        for hh in range(2):
            s_slots[slot][hh] = _dot_nt(ka_ref[hh, pl.ds(start, tk), :], qa_ref[hh])

    def tile(j, slot, carry, masked):
        if not masked:
            scores(j + 1, 1 - slot)
        new = []
        for hh in range(2):
            m, acc = carry[hh]
            s = s_slots[slot][hh]
            if masked:
                s = jnp.where(krel + j * tk <= qpos, s, NEG)
            m_new = jnp.maximum(m, jnp.max(s, axis=0, keepdims=True))
            p = jnp.exp(s - m_new).astype(BF16)
            alpha = jnp.exp(m - m_new)
            acc = alpha * acc + _dot(vt_ref[hh, j, 0:vrows, :], p)
            new.append((m_new, acc))
        return tuple(new)

    def pair(jj, carry):
        carry = tile(2 * jj, 0, carry, False)
        return tile(2 * jj + 1, 1, carry, False)

    scores(0, 0)
    one = (jnp.full((1, tq), NEG, F32), jnp.zeros((vrows, tq), F32))
    carry = lax.fori_loop(0, n_full // 2, pair, (one, one))
    j0 = 2 * (n_full // 2)
    carry = lax.cond(
        n_full % 2 == 1,
        lambda c: tile(j0 + 1, 1, tile(j0, 0, c, False), True),
        lambda c: tile(j0, 0, c, True),
        carry)
    outs = []
    for hh in range(2):
        acc = jnp.concatenate([carry[hh][1], jnp.zeros((AUG - vrows, tq), F32)], axis=0)
        acc_t = acc.T
        outs.append(acc_t[:, 0:HEAD_DIM] / acc_t[:, HEAD_DIM:HEAD_DIM + 1])
    o_ref[...] = jnp.concatenate(outs, axis=1)


def _flash(qa, ka, vt, batch, seq):
    tq = Q_TILE
    nq = seq // tq
    n = batch * seq
    return pl.pallas_call(
        _flash_body,
        grid=(batch, N_HEADS // 2, nq),
        in_specs=[
            pl.BlockSpec((2, tq, AUG), lambda b, hp, i: (hp, b * nq + i, 0)),
            pl.BlockSpec((2, seq, AUG), lambda b, hp, i: (hp, b, 0)),
            pl.BlockSpec((2, seq // KV_TILE, AUG, KV_TILE), lambda b, hp, i: (hp, b, 0, 0)),
        ],
        out_specs=pl.BlockSpec((None, tq, 2 * HEAD_DIM), lambda b, hp, i: (hp, b * nq + i, 0)),
        out_shape=jax.ShapeDtypeStruct((N_HEADS // 2, n, 2 * HEAD_DIM), F32),
        scratch_shapes=[pltpu.VMEM((2, KV_TILE, Q_TILE), F32), pltpu.VMEM((2, KV_TILE, Q_TILE), F32)],
        compiler_params=_cparams(("arbitrary", "arbitrary", "arbitrary")),
        name="flash",
    )(qa, ka, vt)


def _sample_pool_body(xp_ref, st_ref, pw_ref, ps_ref, o_ref, *, start):
    xp = xp_ref[...]

    def window_sum(g, w):
        lanes = slice(g * POOL_GROUP, (g + 1) * POOL_GROUP)
        ws = xp[:, lanes]
        for r in range(1, w):
            ws = ws + st_ref[POOL_STATE - r][:, lanes]
        return ws

    o_ref[...] = _pool_mix(xp, window_sum, lambda w: float(min(start + 1, w)), pw_ref, ps_ref)


def _sample_pool(xp_s, state_t, pw, ps, start):
    return pl.pallas_call(
        functools.partial(_sample_pool_body, start=start),
        out_shape=jax.ShapeDtypeStruct(xp_s.shape, F32),
        compiler_params=pltpu.CompilerParams(vmem_limit_bytes=VMEM_LIMIT),
        name="sample_pool",
    )(xp_s, state_t, pw, ps)


def _sample_attn_body(pt_ref, q_ref, k_ref, v_ref, lf_ref, sfx_ref, *refs, n_steps):
    g_pages = PAGES_PER_STEP
    ck = refs[0:g_pages]
    cv = refs[g_pages:2 * g_pages]
    cl = refs[2 * g_pages:3 * g_pages]
    o_ref = refs[3 * g_pages]
    m_ref, l_ref, acc_ref, car_ref = refs[3 * g_pages + 1:]
    rows = 2 * N_HEADS
    step = pl.program_id(1)

    hrow = lax.broadcasted_iota(jnp.int32, (rows, D_ATTN), 0)
    hcol = lax.broadcasted_iota(jnp.int32, (rows, D_ATTN), 1) // HEAD_DIM
    headmask = hrow == hcol
    qf = q_ref[0].astype(BF16).astype(F32)
    qbd = jnp.where(headmask, qf * ATTN_SCALE, 0.0)

    @pl.when(step == 0)
    def _():
        m_ref[...] = jnp.full(m_ref.shape, NEG, F32)
        l_ref[...] = jnp.zeros_like(l_ref)
        acc_ref[...] = jnp.zeros_like(acc_ref)
        eye = (lax.broadcasted_iota(jnp.int32, (rows, LANES), 0)
               == lax.broadcasted_iota(jnp.int32, (rows, LANES), 1))
        c_new = jnp.sum(jnp.where(eye, lf_ref[0], 0.0), axis=1, keepdims=True)
        car_ref[...] = jnp.broadcast_to(c_new, car_ref.shape)

    carry = car_ref[...]
    sfx = sfx_ref[...]
    bias = []
    for g in range(g_pages):
        lft = cl[g][...]
        hi, mid, lo = _split3(jnp.concatenate([lft, jnp.zeros_like(lft)], axis=0))
        r = _dot(hi, sfx) + _dot(mid, sfx) + _dot(lo, sfx)
        bias.append(r[:, 0:PAGE_SIZE] + carry)
        carry = carry + r[:, PAGE_SIZE:2 * PAGE_SIZE]
    car_ref[...] = carry
    kt_all = jnp.concatenate(
        [ck[g][...].reshape(D_ATTN, PAGE_SIZE).astype(BF16) for g in range(g_pages)], axis=1)
    vt_all = jnp.concatenate(
        [cv[g][...].reshape(D_ATTN, PAGE_SIZE).astype(BF16) for g in range(g_pages)], axis=1)
    s_all = _dot(qbd.astype(BF16), kt_all) + jnp.concatenate(bias, axis=1)
    m_old = m_ref[...]
    m_new = jnp.maximum(m_old, jnp.max(s_all, axis=1, keepdims=True))
    p = jnp.exp(s_all - m_new)
    alpha = jnp.exp(m_old - m_new)
    l_ref[...] = alpha * l_ref[...] + jnp.sum(p, axis=1, keepdims=True)
    acc_ref[...] = alpha * acc_ref[...] + _dot_nt(p.astype(BF16), vt_all)
    m_ref[...] = m_new

    @pl.when(step == n_steps - 1)
    def _():
        kf = k_ref[0].astype(BF16).astype(F32)
        vf = v_ref[0].astype(BF16).astype(F32)
        s_new = jnp.sum(qbd * kf, axis=1, keepdims=True)
        m_old = m_ref[...]
        m_fin = jnp.maximum(m_old, s_new)
        a = jnp.exp(m_old - m_fin)
        p_new = jnp.exp(s_new - m_fin)
        l_fin = a * l_ref[...] + p_new
        acc = a * acc_ref[...] + p_new.astype(BF16).astype(F32) * vf
        o_ref[0] = jnp.sum(jnp.where(headmask, acc / l_fin, 0.0), axis=0, keepdims=True)


def _sample_attn(page_table, q_s, k_s, v_s, lf_s, sfx, cache_kt, cache_vt, cache_lft, layer):
    n_seq, n_pages = page_table.shape
    g_pages = PAGES_PER_STEP
    n_steps = n_pages // g_pages
    rows = 2 * N_HEADS

    def page_map(g, nd):
        def index(b, s, pt):
            return (layer, pt[b, n_pages - 1 - (s * g_pages + g)]) + (0,) * nd
        return index

    tok = lambda b, s, pt: (b, 0, 0)
    in_specs = [
        pl.BlockSpec((1, 1, D_ATTN), tok),
        pl.BlockSpec((1, 1, D_ATTN), tok),
        pl.BlockSpec((1, 1, D_ATTN), tok),
        pl.BlockSpec((1, 1, LANES), tok),
        pl.BlockSpec(sfx.shape, lambda b, s, pt: (0, 0)),
    ]
    page = (None, None, N_HEADS, HEAD_DIM, PAGE_SIZE)
    in_specs += [pl.BlockSpec(page, page_map(g, 3)) for g in range(g_pages)]
    in_specs += [pl.BlockSpec(page, page_map(g, 3)) for g in range(g_pages)]
    in_specs += [pl.BlockSpec((None, None, N_HEADS, PAGE_SIZE), page_map(g, 2)) for g in range(g_pages)]
    grid_spec = pltpu.PrefetchScalarGridSpec(
        num_scalar_prefetch=1,
        grid=(n_seq, n_steps),
        in_specs=in_specs,
        out_specs=pl.BlockSpec((1, 1, D_ATTN), tok),
        scratch_shapes=[
            pltpu.VMEM((rows, 1), F32),
            pltpu.VMEM((rows, 1), F32),
            pltpu.VMEM((rows, D_ATTN), F32),
            pltpu.VMEM((rows, LANES), F32),
        ],
    )
    return pl.pallas_call(
        functools.partial(_sample_attn_body, n_steps=n_steps),
        grid_spec=grid_spec,
        out_shape=jax.ShapeDtypeStruct((n_seq, 1, D_ATTN), F32),
        compiler_params=_cparams(("arbitrary", "arbitrary")),
        name="sample_attn",
    )(page_table, q_s, k_s, v_s, lf_s, sfx,
      *([cache_kt] * g_pages), *([cache_vt] * g_pages), *([cache_lft] * g_pages))


def _outproj_body(x_ref, po_ref, at_ref, wo_ref, g_ref, x1_ref, xn_ref):
    mix = jnp.concatenate([po_ref[...]] + [at_ref[hp] for hp in range(N_HEADS // 2)],
                          axis=1).astype(BF16)
    x1 = x_ref[...] + _dot(mix, wo_ref[...])
    x1_ref[...] = x1
    ms = jnp.mean(x1 * x1, axis=-1, keepdims=True)
    xn_ref[...] = (x1 * lax.rsqrt(ms + EPS) * g_ref[...]).astype(BF16)


def _outproj(x, po, at, wo, g):
    n = x.shape[0]
    t = min(TOK_TILE, n)
    row = lambda i: (i, 0)
    fix = lambda i: (0, 0)
    return pl.pallas_call(
        _outproj_body,
        grid=(n // t,),
        in_specs=[
            pl.BlockSpec((t, D_MODEL), row),
            pl.BlockSpec((t, D_POOL), row),
            pl.BlockSpec((N_HEADS // 2, t, 2 * HEAD_DIM), lambda i: (0, i, 0)),
            pl.BlockSpec(wo.shape, fix),
            pl.BlockSpec((1, D_MODEL), fix),
        ],
        out_specs=[pl.BlockSpec((t, D_MODEL), row), pl.BlockSpec((t, D_MODEL), row)],
        out_shape=[jax.ShapeDtypeStruct((n, D_MODEL), F32), jax.ShapeDtypeStruct((n, D_MODEL), BF16)],
        compiler_params=_cparams(("arbitrary",)),
        name="outproj",
    )(x, po, at, wo, g)


def _top16_ranks(s, sv_ref, half, rowi):
    rank = jnp.full(s.shape, float(PEER_TOPK), F32)
    for c in range(PEER_TOPK):
        m = jnp.max(s, axis=0, keepdims=True)
        first = jnp.min(jnp.where(s == m, rowi, float(N_KEYS)), axis=0, keepdims=True)
        sel = rowi == first
        rank = jnp.where(sel, float(c), rank)
        s = jnp.where(sel, -jnp.inf, s)
        sv_ref[half, c:c + 1, :] = m
    return rank


def _top16_ranks_distinct(halves, sv_ref):
    ss = list(halves)
    ranks = [jnp.full(s.shape, float(PEER_TOPK), F32) for s in ss]
    for c in range(PEER_TOPK):
        for half in range(len(ss)):
            m = jnp.max(ss[half], axis=0, keepdims=True)
            sel = ss[half] == m
            ranks[half] = jnp.where(sel, float(c), ranks[half])
            ss[half] = jnp.where(sel, -jnp.inf, ss[half])
            sv_ref[half, c:c + 1, :] = m
    return ranks


def _pair_segments():
    segs, row = [], 0
    for c in range(PEER_TOPK):
        nd = PEER_TOPK // (c + 1)
        if nd < 8:
            break
        segs.append((row, c, nd))
        row += nd
    bins = []
    for c in range(len(segs), PEER_TOPK):
        nd = PEER_TOPK // (c + 1)
        for bn in bins:
            if bn[0] + nd <= 8:
                bn[1].append((bn[0], c, nd))
                bn[0] += nd
                break
        else:
            bins.append([nd, [(0, c, nd)]])
    for bn in bins:
        segs += [(row + o, c, nd) for o, c, nd in bn[1]]
        row += 8
    return segs, row


def _peer_route_body(xn_ref, wq_ref, sk_ref, flat_ref, a_ref, n_ref, b_ref, r_ref,
                     q_scr, sv_ref, rk_ref, cand_scr, wgt_scr):
    t = xn_ref.shape[0]
    q = _dot(xn_ref[...], wq_ref[...]).astype(BF16)
    for hp in range(2 * PEER_HEADS):
        q_scr[hp] = q[:, hp * D_HALF:(hp + 1) * D_HALF]
    rowi = lax.broadcasted_iota(jnp.int32, (N_KEYS, t), 0).astype(F32)
    segs, n_rows = _pair_segments()
    flat = flat_ref[...]
    big = float(PEER_TOPK * PEER_TOPK)
    cand_scr[...] = jnp.full(cand_scr.shape, -jnp.inf, F32)
    wgt_scr[...] = jnp.zeros_like(wgt_scr)

    def head(h, _):
        s1 = _dot_nt(sk_ref[0], q_scr[2 * h])
        s2 = _dot_nt(sk_ref[1], q_scr[2 * h + 1])
        rk_ref[0], rk_ref[1] = _top16_ranks_distinct((s1, s2), sv_ref)
        ranked = (jnp.sum(jnp.where(rk_ref[0] < float(PEER_TOPK), 1.0, 0.0), axis=0, keepdims=True)
                  + jnp.sum(jnp.where(rk_ref[1] < float(PEER_TOPK), 1.0, 0.0), axis=0, keepdims=True))

        @pl.when(jnp.max(jnp.abs(ranked - float(2 * PEER_TOPK))) > 0.0)
        def _():
            rk_ref[0] = _top16_ranks(s1, sv_ref, 0, rowi)
            rk_ref[1] = _top16_ranks(s2, sv_ref, 1, rowi)

        r1 = rk_ref[0]
        r2 = rk_ref[1]
        sv1 = sv_ref[0]
        sv2 = sv_ref[1]
        e1 = jnp.exp(sv1 - sv1[0:1, :])
        e2 = jnp.exp(sv2 - sv2[0:1, :])
        for row, c, nd in segs:
            cand_scr[row:row + nd, :] = sv1[c:c + 1, :] + sv2[0:nd, :]
            wgt_scr[row:row + nd, :] = e1[c:c + 1, :] * e2[0:nd, :]
        cand = cand_scr[...]
        chosen = jnp.zeros(cand.shape, F32)
        for _k in range(PEER_TOPK):
            m = jnp.max(cand, axis=0, keepdims=True)
            first = jnp.min(jnp.where(cand == m, flat, big), axis=0, keepdims=True)
            sel = flat == first
            chosen = jnp.where(sel, 1.0, chosen)
            cand = jnp.where(sel, -jnp.inf, cand)
        z = jnp.sum(chosen * wgt_scr[...], axis=0, keepdims=True)
        nrow = jnp.zeros((N_KEYS, t), F32)
        for row, c, nd in segs:
            n_c = jnp.sum(chosen[row:row + nd, :], axis=0, keepdims=True)
            nrow = jnp.where(r1 == float(c), n_c, nrow)
        a_ref[h] = jnp.where(r1 < float(PEER_TOPK), jnp.exp(s1 - sv1[0:1, :]) / z, 0.0)
        n_ref[h] = nrow
        b_ref[h] = jnp.exp(s2 - sv2[0:1, :]).astype(BF16)
        r_ref[h] = r2.astype(BF16)
        return 0

    lax.fori_loop(0, PEER_HEADS, head, 0)


def _peer_route(xn, wq, sk):
    n = xn.shape[0]
    t = ROUTE_TILE
    fac = pl.BlockSpec((None, PEER_HEADS, N_KEYS, t), lambda i: (i, 0, 0, 0))
    shp = jax.ShapeDtypeStruct((n // t, PEER_HEADS, N_KEYS, t), F32)
    shp_bf = jax.ShapeDtypeStruct((n // t, PEER_HEADS, N_KEYS, t), BF16)
    segs, n_rows = _pair_segments()
    flat = np.full((n_rows, t), 1e9, np.float32)
    for row, c, nd in segs:
        flat[row:row + nd, :] = (c * PEER_TOPK + np.arange(nd, dtype=np.float32))[:, None]
    return pl.pallas_call(
        _peer_route_body,
        grid=(n // t,),
        in_specs=[
            pl.BlockSpec((t, D_MODEL), lambda i: (i, 0)),
            pl.BlockSpec(wq.shape, lambda i: (0, 0)),
            pl.BlockSpec(sk.shape, lambda i: (0, 0, 0)),
            pl.BlockSpec(flat.shape, lambda i: (0, 0)),
        ],
        out_specs=[fac, fac, fac, fac],
        out_shape=[shp, shp, shp_bf, shp_bf],
        scratch_shapes=[
            pltpu.VMEM((2 * PEER_HEADS, t, D_HALF), BF16),
            pltpu.VMEM((2, PEER_TOPK, t), F32),
            pltpu.VMEM((2, N_KEYS, t), F32),
            pltpu.VMEM((n_rows, t), F32),
            pltpu.VMEM((n_rows, t), F32),
        ],
        compiler_params=_cparams(("arbitrary",)),
        name="peer_route",
    )(xn, wq, sk, jnp.asarray(flat))


def _peer_expert_body(xn_ref, a_ref, n_ref, b_ref, r_ref, u_ref, vt_ref, x1_ref,
                      o_ref, acc_ref, act0_ref, act1_ref, *, n_blocks):
    s = pl.program_id(1)
    acts = (act0_ref, act1_ref)

    def project(slot):
        acts[slot][...] = _dot_nt(u_ref[...], xn_ref[...])

    def combine(slot):
        a_rows = a_ref[...].astype(BF16)
        n_rows = n_ref[...].astype(BF16)
        zero = jnp.zeros((), BF16)
        rtile = a_ref.shape[-1]
        parts = []
        for i in range(EXP_ROWS):
            row = []
            for k in range(a_ref.shape[0]):
                act = acts[slot][i * N_KEYS:(i + 1) * N_KEYS, k * rtile:(k + 1) * rtile]
                ge = jax.nn.gelu(act.astype(BF16), approximate=True)
                gate = None
                for h in range(PEER_HEADS):
                    keep = r_ref[k, h] < n_rows[k, h, i:i + 1, :]
                    term = jnp.where(keep, b_ref[k, h], zero) * a_rows[k, h, i:i + 1, :]
                    gate = term if gate is None else gate + term
                row.append(gate * ge)
            parts.append(jnp.concatenate(row, axis=1))
        w = jnp.concatenate(parts, axis=0)
        acc_ref[...] += _dot(vt_ref[...], w)

    first = s == 0
    last = s == n_blocks
    even = s % 2 == 0

    @pl.when(first)
    def _():
        acc_ref[...] = jnp.zeros_like(acc_ref)
        project(0)

    @pl.when(jnp.logical_and(jnp.logical_not(first), jnp.logical_and(jnp.logical_not(last), even)))
    def _():
        project(0)
        combine(1)

    @pl.when(jnp.logical_and(jnp.logical_not(last), jnp.logical_not(even)))
    def _():
        project(1)
        combine(0)

    @pl.when(last)
    def _():
        combine((n_blocks - 1) % 2)
        o_ref[...] = x1_ref[...] + acc_ref[...].T


def _peer_expert(xn, fa, fn, fb, fr, u, vt, x1, layer):
    n = xn.shape[0]
    t = min(TOK_TILE, n)
    eb = EXP_ROWS * N_KEYS
    n_blocks = u.shape[1] // eb
    rt = t // ROUTE_TILE
    tok = lambda i, s: (i, 0)
    proj = lambda s: jnp.minimum(s, n_blocks - 1)
    comb = lambda s: jnp.maximum(s - 1, 0)
    return pl.pallas_call(
        functools.partial(_peer_expert_body, n_blocks=n_blocks),
        grid=(n // t, n_blocks + 1),
        in_specs=[
            pl.BlockSpec((t, D_MODEL), tok),
            pl.BlockSpec((rt, PEER_HEADS, EXP_ROWS, ROUTE_TILE), lambda i, s: (i, 0, comb(s), 0)),
            pl.BlockSpec((rt, PEER_HEADS, EXP_ROWS, ROUTE_TILE), lambda i, s: (i, 0, comb(s), 0)),
            pl.BlockSpec((rt, PEER_HEADS, N_KEYS, ROUTE_TILE), lambda i, s: (i, 0, 0, 0)),
            pl.BlockSpec((rt, PEER_HEADS, N_KEYS, ROUTE_TILE), lambda i, s: (i, 0, 0, 0)),
            pl.BlockSpec((None, eb, D_MODEL), lambda i, s: (layer, proj(s), 0)),
            pl.BlockSpec((None, None, D_MODEL, eb), lambda i, s: (layer, comb(s), 0, 0)),
            pl.BlockSpec((t, D_MODEL), tok),
        ],
        out_specs=pl.BlockSpec((t, D_MODEL), tok),
        out_shape=jax.ShapeDtypeStruct((n, D_MODEL), F32),
        scratch_shapes=[pltpu.VMEM((D_MODEL, t), F32), pltpu.VMEM((eb, t), F32),
                        pltpu.VMEM((eb, t), F32)],
        compiler_params=_cparams(("arbitrary", "arbitrary")),
        name="peer_expert",
    )(xn, fa, fn, fb, fr, u, vt, x1)


def _selectors():
    wide = D_ATTN + 4 * LANES
    selq = np.zeros((N_HEADS, wide, AUG), np.float32)
    selk = np.zeros((N_HEADS, wide, AUG), np.float32)
    selv = np.zeros((N_HEADS, AUG, D_ATTN + LANES), np.float32)
    ones_row = D_ATTN + 3 * LANES
    for h in range(N_HEADS):
        for d in range(HEAD_DIM):
            selq[h, h * HEAD_DIM + d, d] = ATTN_SCALE
            selk[h, h * HEAD_DIM + d, d] = 1.0
            selv[h, d, h * HEAD_DIM + d] = 1.0
        for piece in range(3):
            selq[h, D_ATTN + piece * LANES + h, HEAD_DIM + piece] = 1.0
            selq[h, ones_row, HEAD_DIM + 3 + piece] = 1.0
            selk[h, ones_row, HEAD_DIM + piece] = 1.0
            selk[h, D_ATTN + piece * LANES + h, HEAD_DIM + 3 + piece] = -1.0
        selv[h, HEAD_DIM, D_ATTN] = 1.0
    hp = N_HEADS // 2
    selq = selq.reshape(hp, 2, wide, AUG).transpose(0, 2, 1, 3).reshape(hp, wide, 2 * AUG)
    selk = selk.reshape(hp, 2, wide, AUG).transpose(0, 2, 1, 3).reshape(hp, wide, 2 * AUG)
    selv = selv.reshape(hp, 2 * AUG, D_ATTN + LANES)
    return (jnp.asarray(selq, BF16), jnp.asarray(selk, BF16), jnp.asarray(selv, BF16))


def kernel(x_prompt, x_sample, cache_k, cache_v, cache_logf, state_pool, page_table,
           norm_mix, w_in, b_forget, q_norm, k_norm, pool_w, pool_scale, w_out,
           norm_ffn, peer_wq, peer_subkeys, peer_u, peer_v):
    batch, seq, _ = x_prompt.shape
    n_seq, dec_seq, _ = x_sample.shape
    depth = w_in.shape[0]
    n_pool = cache_k.shape[1]
    n_pages = page_table.shape[1]
    past = n_pages * PAGE_SIZE
    assert dec_seq == 1 and seq % KV_TILE == 0 and n_pages % PAGES_PER_STEP == 0
    assert SEQ_TILE == KV_TILE and TOK_TILE % ROUTE_TILE == 0
    assert n_seq % 8 == 0
    n_prompt = batch * seq
    assert n_prompt % TOK_TILE == 0
    s_tile = ROUTE_TILE if n_seq <= ROUTE_TILE else TOK_TILE
    n_samp = -(-n_seq // s_tile) * s_tile
    hp = N_HEADS // 2

    def pad_rows(t, axis=0):
        widths = [(0, 0)] * t.ndim
        widths[axis] = (0, n_samp - n_seq)
        return jnp.pad(t, widths)

    x_p = x_prompt.reshape(n_prompt, D_MODEL)
    x_s = pad_rows(x_sample.reshape(n_seq, D_MODEL))
    selq, selk, selv = _selectors()
    hm = jnp.asarray(np.kron(np.eye(N_HEADS), np.full((HEAD_DIM, HEAD_DIM), 1.0 / HEAD_DIM)), BF16)
    jj = np.arange(PAGE_SIZE)
    sfx = jnp.asarray(np.concatenate([(jj[:, None] > jj[None, :]).astype(np.float32),
                                      np.ones((PAGE_SIZE, PAGE_SIZE), np.float32)], axis=1), BF16)
    cache_kt = jnp.transpose(cache_k, (0, 1, 3, 4, 2))
    cache_vt = jnp.transpose(cache_v, (0, 1, 3, 4, 2))
    cache_lft = jnp.transpose(cache_logf, (0, 1, 3, 2))

    eb = EXP_ROWS * N_KEYS
    u_all = peer_u.astype(BF16)
    vt_all = jnp.transpose(peer_v.reshape(depth, -1, eb, D_MODEL), (0, 1, 3, 2)).astype(BF16)

    kp_l, vp_l, fp_l, pp_l, ks_l, vs_l, fs_l, ps_l = [], [], [], [], [], [], [], []
    for l in range(depth):
        wm = w_in[l][:, :D_POOL + 3 * D_ATTN].astype(BF16)
        wf = jnp.pad(w_in[l][:, D_POOL + 3 * D_ATTN:], ((0, 0), (0, LANES - N_HEADS))).astype(BF16)
        bfp = jnp.pad(b_forget[l], (0, LANES - N_HEADS)).reshape(1, LANES)
        inproj = lambda t: _inproj(t, norm_mix[l].reshape(1, D_MODEL), wm, wf, bfp,
                                   jnp.tile(q_norm[l], N_HEADS).reshape(1, D_ATTN),
                                   jnp.tile(k_norm[l], N_HEADS).reshape(1, D_ATTN), hm)
        xp, q, k, v, lf = inproj(x_p)
        xp_s, q_s, k_s, v_s, lf_s = (t[:n_seq] for t in inproj(x_s))
        pw = pool_w[l].astype(BF16)
        ps = pool_scale[l].reshape(1, D_POOL)

        po_p, qa, ka, vt = _poolprep(xp, lf, q, k, v, pw, ps, selq, selk, selv, batch, seq)
        at_p = _flash(qa, ka, vt, batch, seq)

        po_s = _sample_pool(xp_s, jnp.transpose(state_pool[l], (1, 0, 2)), pw, ps, past)
        row = lambda t: t.reshape(n_seq, 1, t.shape[1])
        at_s = _sample_attn(page_table, row(q_s), row(k_s), row(v_s), row(lf_s), sfx,
                            cache_kt, cache_vt, cache_lft, l).reshape(n_seq, hp, 2 * HEAD_DIM)
        at_s = pad_rows(jnp.transpose(at_s, (1, 0, 2)), axis=1)

        wo = w_out[l].astype(BF16)
        gf = norm_ffn[l].reshape(1, D_MODEL)
        wq = peer_wq[l].astype(BF16)
        sk = peer_subkeys[l].astype(BF16)

        def channel_mix(x_in, po, at, layer=l):
            x1, xn = _outproj(x_in, po, at, wo, gf)
            fa, fn, fb, fr = _peer_route(xn, wq, sk)
            return _peer_expert(xn, fa, fn, fb, fr, u_all, vt_all, x1, layer)

        x_p = channel_mix(x_p, po_p, at_p)
        x_s = channel_mix(x_s, pad_rows(po_s), at_s)

        kp_l.append(k.reshape(batch, seq, N_HEADS, HEAD_DIM))
        vp_l.append(v.reshape(batch, seq, N_HEADS, HEAD_DIM))
        fp_l.append(lf[:, :N_HEADS].reshape(batch, seq, N_HEADS))
        pp_l.append(xp.reshape(batch, seq, D_POOL)[:, seq - POOL_STATE:])
        ks_l.append(k_s.reshape(n_seq, 1, N_HEADS, HEAD_DIM))
        vs_l.append(v_s.reshape(n_seq, 1, N_HEADS, HEAD_DIM))
        fs_l.append(lf_s[:, :N_HEADS].reshape(n_seq, 1, N_HEADS))
        ps_l.append(jnp.concatenate([state_pool[l][:, 1:], xp_s[:, None, :]], axis=1))

    return (x_p.reshape(batch, seq, D_MODEL), x_s[:n_seq].reshape(n_seq, 1, D_MODEL),
            jnp.stack(kp_l), jnp.stack(vp_l), jnp.stack(fp_l), jnp.stack(pp_l),
            jnp.stack(ks_l), jnp.stack(vs_l), jnp.stack(fs_l), jnp.stack(ps_l))
```
